```python
import math
import jax, jax.numpy as jnp
from jax import lax
import numpy as np

D_MODEL = 1024
BATCH = 2
SEQ = 8192
DEPTH = 2

N_MIXERS = 2
N_A = (DEPTH + 1) // 2
N_B = DEPTH // 2

D_RNN = 1280
LRU_BLOCKS = 16
LRU_BLOCK_W = D_RNN // LRU_BLOCKS
LRU_C = 8.0
LRU_CONV_W = 4

ATTN_GROUPS = ((128, 1), (512, 4), (2048, 16))
N_GROUPS = len(ATTN_GROUPS)
N_HEADS = 16
HEAD_DIM = D_MODEL // N_HEADS
D_ATTN = N_HEADS * HEAD_DIM

D_FF = 2816
FF_CONV_W = 3

LN_EPS = 1e-5
DN_ALPHA = (2 * DEPTH) ** 0.25
DN_BETA = (8 * DEPTH) ** -0.25
NEG = -1e30

kernel_name = "hybrid_rglru_dilated_attn_encoder"


def layer_norm(z, g, b):
    zf = z.astype(jnp.float32)
    mu = jnp.mean(zf, axis=-1, keepdims=True)
    var = jnp.mean(jnp.square(zf - mu), axis=-1, keepdims=True)
    return ((zf - mu) * lax.rsqrt(var + LN_EPS) * g + b).astype(z.dtype)


def depthwise_conv(z, w, b, left):
    K = w.shape[0]
    S = z.shape[1]
    zp = jnp.pad(z, ((0, 0), (left, K - 1 - left), (0, 0)))
    y = zp[:, 0:S] * w[0]
    for k in range(1, K):
        y = y + zp[:, k:k + S] * w[k]
    return y + b


def lru_combine(left, right):
    a1, b1 = left
    a2, b2 = right
    return a1 * a2, a2 * b1 + b2


def rglru_scan(u, ub, w_a, b_a, w_x, b_x, lam, reverse):
    B, S, _ = u.shape
    r = jax.nn.sigmoid(jnp.einsum('bsnc,ncd->bsnd', ub, w_a).reshape(B, S, D_RNN) + b_a).astype(jnp.float32)
    i = jax.nn.sigmoid(jnp.einsum('bsnc,ncd->bsnd', ub, w_x).reshape(B, S, D_RNN) + b_x).astype(jnp.float32)
    log_a = -LRU_C * jax.nn.softplus(-lam.astype(jnp.float32)) * r
    a = jnp.exp(log_a)
    inp = jnp.sqrt(-jnp.expm1(2.0 * log_a)) * (i * u.astype(jnp.float32))
    _, h = lax.associative_scan(lru_combine, (a, inp), axis=1, reverse=reverse)
    return h


def rglru_mixer(x, w_in, conv_w, conv_b, w_a, b_a, w_x, b_x, lam, w_out):
    B, S, _ = x.shape
    gate, u = jnp.split(x @ w_in, 2, axis=-1)
    u = depthwise_conv(u, conv_w, conv_b, LRU_CONV_W // 2)
    ub = u.reshape(B, S, LRU_BLOCKS, LRU_BLOCK_W)
    h_fwd = rglru_scan(u, ub, w_a[0], b_a[0], w_x[0], b_x[0], lam[0], False)
    h_bwd = rglru_scan(u, ub, w_a[1], b_a[1], w_x[1], b_x[1], lam[1], True)
    y = jax.nn.gelu(gate) * (h_fwd + h_bwd).astype(x.dtype)
    return y @ w_out


def alibi_slopes():
    return jnp.exp2(-8.0 * jnp.arange(1, N_HEADS + 1, dtype=jnp.float32) / N_HEADS)


def dilated_band_attention(q, k, v, window, dil, slopes):
    B, S, H, Dh = q.shape
    half = window // (2 * dil)
    blk = half
    L = S // dil
    nb = -(-L // blk)
    Lp = nb * blk

    def to_blocks(z):
        z = z.reshape(B, L, dil, H, Dh).transpose(0, 2, 1, 3, 4)
        z = jnp.pad(z, ((0, 0), (0, 0), (0, Lp - L), (0, 0), (0, 0)))
        return z.reshape(B, dil, nb, blk, H, Dh)

    def neighbours(z):
        zp = jnp.pad(z, ((0, 0), (0, 0), (1, 1), (0, 0), (0, 0), (0, 0)))
        return jnp.concatenate([zp[:, :, :-2], zp[:, :, 1:-1], zp[:, :, 2:]], axis=3)

    qb = to_blocks(q)
    kb = neighbours(to_blocks(k))
    vb = neighbours(to_blocks(v))

    rel = jnp.arange(3 * blk)[None, :] - blk - jnp.arange(blk)[:, None]
    kidx = (jnp.arange(nb)[:, None] - 1) * blk + jnp.arange(3 * blk)[None, :]
    valid = (jnp.abs(rel) <= half)[None] & ((kidx >= 0) & (kidx < L))[:, None, :]
    bias = -slopes[:, None, None] * (jnp.abs(rel) * dil).astype(jnp.float32)[None]

    s = jnp.einsum('brnihd,brnjhd->brnhij', qb, kb,
                   preferred_element_type=jnp.float32) * (Dh ** -0.5) + bias
    s = jnp.where(valid[:, None], s, NEG)
    m = jnp.max(s, axis=-1, keepdims=True)
    p = jnp.exp(s - m)
    den = jnp.sum(p, axis=-1, keepdims=True)
    o = jnp.einsum('brnhij,brnjhd->brnihd', p, vb.astype(jnp.float32))
    o = o * jnp.swapaxes(1.0 / den, 3, 4)
    lse = (m + jnp.log(den))[..., 0]

    o = o.reshape(B, dil, Lp, H, Dh)[:, :, :L].transpose(0, 2, 1, 3, 4).reshape(B, S, H, Dh)
    lse = jnp.swapaxes(lse, 3, 4).reshape(B, dil, Lp, H)[:, :, :L].transpose(0, 2, 1, 3).reshape(B, S, H)
    return o, lse


def dilated_attention_mixer(x, w_qkv, w_o):
    B, S, _ = x.shape
    qkv = (x @ w_qkv).reshape(B, S, N_GROUPS, 3, N_HEADS, HEAD_DIM)
    slopes = alibi_slopes()
    outs, lses = [], []
    for gi, (window, dil) in enumerate(ATTN_GROUPS):
        o, lse = dilated_band_attention(qkv[:, :, gi, 0], qkv[:, :, gi, 1], qkv[:, :, gi, 2],
                                        window, dil, slopes)
        outs.append(o)
        lses.append(lse)
    wts = jax.nn.softmax(jnp.stack(lses), axis=0)
    o = jnp.einsum('gbsh,gbshd->bshd', wts, jnp.stack(outs))
    return o.reshape(B, S, D_ATTN).astype(x.dtype) @ w_o


def conv_ffn(x, w_up, conv_w, conv_b, w_down):
    v, g = jnp.split(x @ w_up, 2, axis=-1)
    g = depthwise_conv(g, conv_w, conv_b, FF_CONV_W // 2)
    return (jax.nn.gelu(g) * v) @ w_down


def setup_inputs(seed: int = 0) -> dict:
    key = jax.random.key(seed)
    ks = jax.random.split(key, 20)
    f32 = jnp.float32
    nrm = lambda k, shape, scale: jax.random.normal(k, shape, f32) * scale
    u = jax.random.uniform(ks[9], (N_A, 2, D_RNN), f32, 0.9, 0.999)
    a0 = u ** (1.0 / LRU_C)
    return {
        "x": jax.random.normal(ks[0], (BATCH, SEQ, D_MODEL), f32),
        "ln_g": 1.0 + nrm(ks[1], (DEPTH, 2, D_MODEL), 0.02),
        "ln_b": nrm(ks[2], (DEPTH, 2, D_MODEL), 0.02),
        "rg_w_in": nrm(ks[3], (N_A, D_MODEL, 2 * D_RNN), D_MODEL ** -0.5),
        "rg_conv_w": nrm(ks[4], (N_A, LRU_CONV_W, D_RNN), LRU_CONV_W ** -0.5),
        "rg_conv_b": nrm(ks[5], (N_A, D_RNN), 0.01),
        "rg_w_a": nrm(ks[6], (N_A, 2, LRU_BLOCKS, LRU_BLOCK_W, LRU_BLOCK_W), LRU_BLOCK_W ** -0.5),
        "rg_b_a": nrm(ks[7], (N_A, 2, D_RNN), 0.01),
        "rg_w_x": nrm(ks[8], (N_A, 2, LRU_BLOCKS, LRU_BLOCK_W, LRU_BLOCK_W), LRU_BLOCK_W ** -0.5),
        "rg_b_x": nrm(ks[10], (N_A, 2, D_RNN), 0.01),
        "rg_lam": jnp.log(a0) - jnp.log1p(-a0),
        "rg_w_out": nrm(ks[11], (N_A, D_RNN, D_MODEL), DN_BETA * D_RNN ** -0.5),
        "at_w_qkv": nrm(ks[12], (N_B, D_MODEL, N_GROUPS * 3 * D_ATTN), D_MODEL ** -0.5),
        "at_w_o": nrm(ks[13], (N_B, D_ATTN, D_MODEL), DN_BETA * D_ATTN ** -0.5),
        "ff_w_up": nrm(ks[14], (DEPTH, D_MODEL, 2 * D_FF), D_MODEL ** -0.5),
        "ff_conv_w": nrm(ks[15], (DEPTH, FF_CONV_W, D_FF), FF_CONV_W ** -0.5),
        "ff_conv_b": nrm(ks[16], (DEPTH, D_FF), 0.01),
        "ff_w_down": nrm(ks[17], (DEPTH, D_FF, D_MODEL), DN_BETA * D_FF ** -0.5),
    }


def reference(x, ln_g, ln_b, rg_w_in, rg_conv_w, rg_conv_b, rg_w_a, rg_b_a, rg_w_x, rg_b_x,
              rg_lam, rg_w_out, at_w_qkv, at_w_o, ff_w_up, ff_conv_w, ff_conv_b, ff_w_down):
    for i in range(DEPTH):
        j = i // N_MIXERS
        if i % N_MIXERS == 0:
            mix = rglru_mixer(x, rg_w_in[j], rg_conv_w[j], rg_conv_b[j], rg_w_a[j], rg_b_a[j],
                              rg_w_x[j], rg_b_x[j], rg_lam[j], rg_w_out[j])
        else:
            mix = dilated_attention_mixer(x, at_w_qkv[j], at_w_o[j])
        x = layer_norm(DN_ALPHA * x + mix, ln_g[i, 0], ln_b[i, 0])
        ffn = conv_ffn(x, ff_w_up[i], ff_conv_w[i], ff_conv_b[i], ff_w_down[i])
        x = layer_norm(DN_ALPHA * x + ffn, ln_g[i, 1], ln_b[i, 1])
    return x
```

```python
import functools
import math

import jax
import jax.numpy as jnp
from jax import lax
from jax.experimental import pallas as pl
from jax.experimental.pallas import tpu as pltpu

F32 = jnp.float32
BF16 = jnp.bfloat16

LRU_C = 8.0
LRU_BLOCKS = 16
ATTN_GROUPS = ((128, 1), (512, 4), (2048, 16))
N_HEADS = 16
HEAD_DIM = 64
LN_EPS = 1e-5
DEPTH = 2
DN_ALPHA = (2 * DEPTH) ** 0.25
NEG = -1e30

VMEM_LIMIT_BYTES = 56 * 1024 * 1024
MXU_TILE = 256
BF16_ROWS = 16
F32_ROWS = 8

_GELU_C = math.sqrt(2.0 / math.pi)


def _gelu(x):
    return 0.5 * x * (1.0 + jnp.tanh(_GELU_C * (x + 0.044715 * (x * x * x))))


def _sigmoid(x):
    return 0.5 * (1.0 + jnp.tanh(0.5 * x))


def _layer_norm(z, g, b):
    mu = jnp.mean(z, axis=-1, keepdims=True)
    zc = z - mu
    var = jnp.mean(zc * zc, axis=-1, keepdims=True)
    return zc * lax.rsqrt(var + LN_EPS) * g + b


def _resident(shape):
    nd = len(shape)
    return pl.BlockSpec(shape, lambda *_: (0,) * nd, pipeline_mode=pl.Buffered(1))


def _params(n_axes):
    return pltpu.CompilerParams(
        dimension_semantics=("arbitrary",) * n_axes,
        vmem_limit_bytes=VMEM_LIMIT_BYTES)


def _in_proj_kernel(x_ref, w_ref, gate_ref, u_ref, xb_ref, *, d_rnn, cn):
    xb_ref[...] = x_ref[...].astype(BF16)
    for j in range(d_rnn // cn):
        g = jnp.dot(xb_ref[...], w_ref[:, j * cn:(j + 1) * cn], preferred_element_type=F32)
        gate_ref[:, j * cn:(j + 1) * cn] = _gelu(g).astype(BF16)
    for j in range(d_rnn // cn):
        u = jnp.dot(xb_ref[...], w_ref[:, d_rnn + j * cn:d_rnn + (j + 1) * cn],
                    preferred_element_type=F32)
        u_ref[:, j * cn:(j + 1) * cn] = u.astype(BF16)


def _in_proj(x2d, w_in, tm=512, cn=256):
    m, d = x2d.shape
    d_rnn = w_in.shape[1] // 2
    return pl.pallas_call(
        functools.partial(_in_proj_kernel, d_rnn=d_rnn, cn=cn),
        grid=(m // tm,),
        in_specs=[pl.BlockSpec((tm, d), lambda i: (i, 0)), _resident(w_in.shape)],
        out_specs=[pl.BlockSpec((tm, d_rnn), lambda i: (i, 0)),
                   pl.BlockSpec((tm, d_rnn), lambda i: (i, 0))],
        out_shape=[jax.ShapeDtypeStruct((m, d_rnn), BF16),
                   jax.ShapeDtypeStruct((m, d_rnn), BF16)],
        scratch_shapes=[pltpu.VMEM((tm, d), BF16)],
        compiler_params=_params(1),
        name="in_proj",
    )(x2d, w_in)


def _gate_k_starts(d_rnn, n_blocks, ct, kw):
    bw = d_rnn // n_blocks
    starts = []
    for j in range(d_rnn // ct):
        lo = (j * ct) // bw * bw
        hi = ((j * ct + ct - 1) // bw + 1) * bw
        k0 = min(lo // 128 * 128, d_rnn - kw)
        assert k0 <= lo and hi <= k0 + kw
        starts.append(k0)
    return tuple(starts)


def _scan_tile(a, b, h0, reverse):
    tt, cw = a.shape
    groups = tt // F32_ROWS
    a3 = a.reshape(groups, F32_ROWS, cw)
    b3 = b.reshape(groups, F32_ROWS, cw)
    row = lax.broadcasted_iota(jnp.int32, a3.shape, 1)
    for k in (1, 2, 4):
        if reverse:
            shift, keep = F32_ROWS - k, row < F32_ROWS - k
        else:
            shift, keep = k, row >= k
        ar = pltpu.roll(a3, shift, 1)
        br = pltpu.roll(b3, shift, 1)
        b3 = b3 + a3 * jnp.where(keep, br, 0.0)
        a3 = a3 * jnp.where(keep, ar, 1.0)
    out = [None] * groups
    h = h0
    for g in (reversed(range(groups)) if reverse else range(groups)):
        hg = b3[g] + a3[g] * h
        out[g] = hg
        h = hg[0:1] if reverse else hg[F32_ROWS - 1:F32_ROWS]
    return jnp.concatenate(out, axis=0), h


def _lru_tile(u_prev_ref, u_cur_ref, u_next_ref, cw_ref, cb_ref, wg_ref, ba_ref, bx_ref,
              lam_ref, carry_ref, emit, *, reverse, n_tiles, k_starts, kw, ct):
    step = pl.program_id(1)
    ti = n_tiles - 1 - step if reverse else step

    @pl.when(step == 0)
    def _():
        carry_ref[...] = jnp.zeros_like(carry_ref)

    tt = u_cur_ref.shape[0]
    halo = u_prev_ref.shape[0]
    prev = u_prev_ref[...].astype(F32) * (ti > 0).astype(F32)
    nxt = u_next_ref[...].astype(F32) * (ti < n_tiles - 1).astype(F32)
    ext = jnp.concatenate([prev, u_cur_ref[...].astype(F32), nxt], axis=0)
    n = tt + 2 * halo
    conv = (pltpu.roll(ext, 2, 0) * cw_ref[0:1, :] + pltpu.roll(ext, 1, 0) * cw_ref[1:2, :]
            + ext * cw_ref[2:3, :] + pltpu.roll(ext, n - 1, 0) * cw_ref[3:4, :]
            + cb_ref[...])[halo:halo + tt]
    ub = conv.astype(BF16)

    lam = lam_ref[...]
    neg_lam = -lam
    softplus = jnp.maximum(neg_lam, 0.0) + jnp.log1p(jnp.exp(-jnp.abs(neg_lam)))
    decay = -LRU_C * softplus

    for j, k0 in enumerate(k_starts):
        cols = slice(j * ct, (j + 1) * ct)
        g = jnp.dot(ub[:, k0:k0 + kw], wg_ref[j], preferred_element_type=F32)
        r = _sigmoid(g[:, :ct] + ba_ref[:, cols])
        i = _sigmoid(g[:, ct:] + bx_ref[:, cols])
        log_a = decay[:, cols] * r
        a = jnp.exp(log_a)
        gain = jnp.sqrt(-jnp.tanh(log_a) * (a * a + 1.0))
        b = gain * (i * conv[:, cols])
        h, h_last = _scan_tile(a, b, carry_ref[0:1, cols], reverse)
        carry_ref[0:1, cols] = h_last
        emit(j, cols, h)


def _lru_fwd_kernel(u_prev_ref, u_cur_ref, u_next_ref, cw_ref, cb_ref, wg_ref, ba_ref, bx_ref,
                    lam_ref, h_ref, carry_ref, **kw):
    def emit(j, cols, h):
        h_ref[:, cols] = h.astype(BF16)

    _lru_tile(u_prev_ref, u_cur_ref, u_next_ref, cw_ref, cb_ref, wg_ref, ba_ref, bx_ref,
              lam_ref, carry_ref, emit, reverse=False, **kw)


def _lru_bwd_kernel(u_prev_ref, u_cur_ref, u_next_ref, cw_ref, cb_ref, wg_ref, ba_ref, bx_ref,
                    lam_ref, hf_ref, gate_ref, x_ref, wout_ref, lng_ref, lnb_ref,
                    out_ref, carry_ref, y_ref, **kw):
    def emit(j, cols, h):
        y = gate_ref[:, cols].astype(F32) * (hf_ref[:, cols].astype(F32) + h)
        y_ref[:, cols] = y.astype(BF16)

    _lru_tile(u_prev_ref, u_cur_ref, u_next_ref, cw_ref, cb_ref, wg_ref, ba_ref, bx_ref,
              lam_ref, carry_ref, emit, reverse=True, **kw)
    mix = jnp.dot(y_ref[...], wout_ref[...], preferred_element_type=F32)
    z = DN_ALPHA * x_ref[...] + mix
    out_ref[...] = _layer_norm(z, lng_ref[...], lnb_ref[...])


def _lru_specs(batch, seq, d_rnn, tt, reverse):
    n_tiles = seq // tt
    halo = BF16_ROWS
    hb = tt // halo
    n_halo = seq // halo

    def t_of(i):
        return n_tiles - 1 - i if reverse else i

    cur = pl.BlockSpec((None, tt, d_rnn), lambda b, i: (b, t_of(i), 0))
    prev = pl.BlockSpec((None, halo, d_rnn),
                        lambda b, i: (b, jnp.maximum(t_of(i) * hb - 1, 0), 0))
    nxt = pl.BlockSpec((None, halo, d_rnn),
                       lambda b, i: (b, jnp.minimum((t_of(i) + 1) * hb, n_halo - 1), 0))
    return n_tiles, cur, prev, nxt


def _lru_fwd(u, conv_w, conv_b, wg, b_a, b_x, lam, k_starts, tt=256, ct=256, kw=512):
    batch, seq, d_rnn = u.shape
    n_tiles, cur, prev, nxt = _lru_specs(batch, seq, d_rnn, tt, False)
    kern = functools.partial(_lru_fwd_kernel, n_tiles=n_tiles, k_starts=k_starts, kw=kw, ct=ct)
    return pl.pallas_call(
        kern,
        grid=(batch, n_tiles),
        in_specs=[prev, cur, nxt, _resident(conv_w.shape), _resident(conv_b.shape),
                  _resident(wg.shape), _resident(b_a.shape), _resident(b_x.shape),
                  _resident(lam.shape)],
        out_specs=cur,
        out_shape=jax.ShapeDtypeStruct((batch, seq, d_rnn), BF16),
        scratch_shapes=[pltpu.VMEM((F32_ROWS, d_rnn), F32)],
        compiler_params=_params(2),
        name="lru_fwd",
    )(u, u, u, conv_w, conv_b, wg, b_a, b_x, lam)


def _lru_bwd(u, conv_w, conv_b, wg, b_a, b_x, lam, h_fwd, gate, x, w_out, ln_g, ln_b,
             k_starts, tt=256, ct=256, kw=512):
    batch, seq, d_rnn = u.shape
    d = x.shape[-1]
    n_tiles, cur, prev, nxt = _lru_specs(batch, seq, d_rnn, tt, True)
    xspec = pl.BlockSpec((None, tt, d), lambda b, i: (b, n_tiles - 1 - i, 0))
    kern = functools.partial(_lru_bwd_kernel, n_tiles=n_tiles, k_starts=k_starts, kw=kw, ct=ct)
    return pl.pallas_call(
        kern,
        grid=(batch, n_tiles),
        in_specs=[prev, cur, nxt, _resident(conv_w.shape), _resident(conv_b.shape),
                  _resident(wg.shape), _resident(b_a.shape), _resident(b_x.shape),
                  _resident(lam.shape), cur, cur, xspec, _resident(w_out.shape),
                  _resident(ln_g.shape), _resident(ln_b.shape)],
        out_specs=xspec,
        out_shape=jax.ShapeDtypeStruct((batch, seq, d), F32),
        scratch_shapes=[pltpu.VMEM((F32_ROWS, d_rnn), F32), pltpu.VMEM((tt, d_rnn), BF16)],
        compiler_params=_params(2),
        name="lru_bwd",
    )(u, u, u, conv_w, conv_b, wg, b_a, b_x, lam, h_fwd, gate, x, w_out, ln_g, ln_b)


def _gate_weights(w_a, w_x, k_starts, kw, ct):
    nb, bw, _ = w_a.shape
    d_rnn = nb * bw
    eye = jnp.eye(nb, dtype=w_a.dtype)

    def dense(w):
        return jnp.einsum('ncd,nm->ncmd', w, eye).reshape(d_rnn, d_rnn)

    wa, wx = dense(w_a), dense(w_x)
    tiles = [jnp.concatenate([wa[k0:k0 + kw, j * ct:(j + 1) * ct],
                              wx[k0:k0 + kw, j * ct:(j + 1) * ct]], axis=1)
             for j, k0 in enumerate(k_starts)]
    return jnp.stack(tiles).astype(BF16)


def _ffn_kernel(x_prev_ref, x_ref, x_next_ref, wup_ref, cw_ref, cb_ref, wdn_ref, lng_ref,
                lnb_ref, out_ref, xe_ref, acc_ref, *, tiles_per_seq, d_ff, cf):
    i = pl.program_id(0)
    tm = x_ref.shape[0]
    halo = x_prev_ref.shape[0]
    first = (i % tiles_per_seq == 0)
    last = (i % tiles_per_seq == tiles_per_seq - 1)
    xe_ref[0:halo, :] = (x_prev_ref[...] * jnp.where(first, 0.0, 1.0)).astype(BF16)
    xe_ref[halo:halo + tm, :] = x_ref[...].astype(BF16)
    xe_ref[halo + tm:, :] = (x_next_ref[...] * jnp.where(last, 0.0, 1.0)).astype(BF16)
    n = tm + 2 * halo
    for c in range(d_ff // cf):
        cols = slice(c * cf, (c + 1) * cf)
        g = jnp.dot(xe_ref[...], wup_ref[:, d_ff + c * cf:d_ff + (c + 1) * cf],
                    preferred_element_type=F32)
        v = jnp.dot(xe_ref[halo:halo + tm, :], wup_ref[:, cols], preferred_element_type=F32)
        gc = (pltpu.roll(g, 1, 0) * cw_ref[0:1, cols] + g * cw_ref[1:2, cols]
              + pltpu.roll(g, n - 1, 0) * cw_ref[2:3, cols] + cb_ref[:, cols])[halo:halo + tm]
        hmid = (_gelu(gc) * v).astype(BF16)
        part = jnp.dot(hmid, wdn_ref[cols, :], preferred_element_type=F32)
        if c == 0:
            acc_ref[...] = part
        else:
            acc_ref[...] += part
    z = DN_ALPHA * x_ref[...] + acc_ref[...]
    out_ref[...] = _layer_norm(z, lng_ref[...], lnb_ref[...])


def _conv_ffn(x2d, seq, w_up, conv_w, conv_b, w_down, ln_g, ln_b, tm=512, cf=256):
    m, d = x2d.shape
    d_ff = w_down.shape[0]
    halo = BF16_ROWS
    hb = tm // halo
    n_halo = m // halo
    kern = functools.partial(_ffn_kernel, tiles_per_seq=seq // tm, d_ff=d_ff, cf=cf)
    return pl.pallas_call(
        kern,
        grid=(m // tm,),
        in_specs=[pl.BlockSpec((halo, d), lambda i: (jnp.maximum(i * hb - 1, 0), 0)),
                  pl.BlockSpec((tm, d), lambda i: (i, 0)),
                  pl.BlockSpec((halo, d), lambda i: (jnp.minimum((i + 1) * hb, n_halo - 1), 0)),
                  _resident(w_up.shape), _resident(conv_w.shape), _resident(conv_b.shape),
                  _resident(w_down.shape), _resident(ln_g.shape), _resident(ln_b.shape)],
        out_specs=pl.BlockSpec((tm, d), lambda i: (i, 0)),
        out_shape=jax.ShapeDtypeStruct((m, d), F32),
        scratch_shapes=[pltpu.VMEM((tm + 2 * halo, d), BF16), pltpu.VMEM((tm, d), F32)],
        compiler_params=_params(1),
        name="conv_ffn",
    )(x2d, x2d, x2d, w_up, conv_w, conv_b, w_down, ln_g, ln_b)


def _qkv_kernel(x_ref, w_ref, out_ref, xb_ref, *, cn):
    xb_ref[...] = x_ref[...].astype(BF16)
    for j in range(w_ref.shape[1] // cn):
        cols = slice(j * cn, (j + 1) * cn)
        out_ref[:, cols] = jnp.dot(xb_ref[...], w_ref[:, cols],
                                   preferred_element_type=F32).astype(BF16)


def _qkv_proj(x2d, w_qkv, tm=512, cn=512):
    m, d = x2d.shape
    n = w_qkv.shape[1]
    return pl.pallas_call(
        functools.partial(_qkv_kernel, cn=cn),
        grid=(m // tm,),
        in_specs=[pl.BlockSpec((tm, d), lambda i: (i, 0)), _resident(w_qkv.shape)],
        out_specs=pl.BlockSpec((tm, n), lambda i: (i, 0)),
        out_shape=jax.ShapeDtypeStruct((m, n), BF16),
        scratch_shapes=[pltpu.VMEM((tm, d), BF16)],
        compiler_params=_params(1),
        name="qkv_proj",
    )(x2d, w_qkv)


def _attn_kernel(q_ref, kp_ref, kc_ref, kn_ref, vp_ref, vc_ref, vn_ref, o_ref, lse_ref,
                 bias_ref, *, n_q, half, dil):
    qi = pl.program_id(1)
    tq = q_ref.shape[0]
    win = tq + 2 * half
    lanes = 2 * HEAD_DIM

    @pl.when(jnp.logical_and(pl.program_id(0) == 0, qi == 0))
    def _():
        row = lax.broadcasted_iota(jnp.int32, (tq, win), 0)
        col = lax.broadcasted_iota(jnp.int32, (tq, win), 1)
        rel = col - half - row
        dist = jnp.abs(rel)
        in_band = dist <= half
        distf = (dist * dil).astype(F32)
        for variant in range(3):
            ok = in_band
            if variant == 0:
                ok = jnp.logical_and(ok, col >= half)
            if variant == 2:
                ok = jnp.logical_and(ok, col < half + tq)
            for h in range(N_HEADS):
                slope = 2.0 ** (-8.0 * (h + 1) / N_HEADS)
                bias_ref[variant, h] = jnp.where(ok, -slope * distf, NEG)

    variant = jnp.where(qi == 0, 0, jnp.where(qi == n_q - 1, 2, 1))
    kwin = jnp.concatenate([kp_ref[...], kc_ref[...], kn_ref[...]], axis=0)
    vwin = jnp.concatenate([vp_ref[...], vc_ref[...], vn_ref[...]], axis=0)
    lane = lax.broadcasted_iota(jnp.int32, (1, lanes), 1)
    lse_lane = lax.broadcasted_iota(jnp.int32, (tq, lanes), 1)
    lse_blk = jnp.zeros((tq, lanes), F32)
    zero = jnp.zeros((), BF16)
    scale = HEAD_DIM ** -0.5
    for hp in range(N_HEADS // 2):
        cols = slice(hp * lanes, (hp + 1) * lanes)
        q2 = q_ref[:, cols] * jnp.asarray(scale, BF16)
        k2 = kwin[:, cols]
        v2 = vwin[:, cols]
        o_pair = jnp.zeros((tq, lanes), F32)
        for side in range(2):
            h = 2 * hp + side
            mine = (lane >= HEAD_DIM) if side else (lane < HEAD_DIM)
            qh = jnp.where(mine, q2, zero)
            vh = jnp.where(mine, v2, zero)
            s = lax.dot_general(qh, k2, (((1,), (1,)), ((), ())), preferred_element_type=F32)
            s = s + bias_ref[variant, h]
            m = jnp.max(s, axis=-1, keepdims=True)
            p = jnp.exp(s - m)
            den = jnp.sum(p, axis=-1, keepdims=True)
            o = jnp.dot(p.astype(BF16), vh, preferred_element_type=F32)
            o_pair = o_pair + o * (1.0 / den)
            lse_blk = jnp.where(lse_lane == h, m + jnp.log(den), lse_blk)
        o_ref[:, cols] = o_pair.astype(BF16)
    lse_ref[...] = lse_blk


def _band_attention(q, k, v, dil, half, tq=128):
    n_seq, length, width = q.shape
    n_q = length // tq
    hb = tq // half
    n_half = length // half
    assert n_q >= 2
    cur = pl.BlockSpec((None, tq, width), lambda s, i: (s, i, 0))
    prev = pl.BlockSpec((None, half, width), lambda s, i: (s, jnp.maximum(i * hb - 1, 0), 0))
    nxt = pl.BlockSpec((None, half, width),
                       lambda s, i: (s, jnp.minimum((i + 1) * hb, n_half - 1), 0))
    lse_spec = pl.BlockSpec((None, tq, 2 * HEAD_DIM), lambda s, i: (s, i, 0))
    kern = functools.partial(_attn_kernel, n_q=n_q, half=half, dil=dil)
    return pl.pallas_call(
        kern,
        grid=(n_seq, n_q),
        in_specs=[cur, prev, cur, nxt, prev, cur, nxt],
        out_specs=[cur, lse_spec],
        out_shape=[jax.ShapeDtypeStruct((n_seq, length, width), BF16),
                   jax.ShapeDtypeStruct((n_seq, length, 2 * HEAD_DIM), F32)],
        scratch_shapes=[pltpu.VMEM((3, N_HEADS, tq, tq + 2 * half), F32)],
        compiler_params=_params(2),
        name=f"band_attn_d{dil}",
    )(q, k, k, k, v, v, v)


def _merge_kernel(o0_ref, o1_ref, o2_ref, l0_ref, l1_ref, l2_ref, x_ref, e_ref, wo_ref,
                  lng_ref, lnb_ref, out_ref):
    lses = [l0_ref[...], l1_ref[...], l2_ref[...]]
    top = jnp.maximum(jnp.maximum(lses[0], lses[1]), lses[2])
    ws = [jnp.exp(l - top) for l in lses]
    inv = 1.0 / (ws[0] + ws[1] + ws[2])
    o = None
    for w, o_ref in zip(ws, (o0_ref, o1_ref, o2_ref)):
        w = w * inv
        w_hi = w.astype(BF16)
        w_lo = (w - w_hi.astype(F32)).astype(BF16)
        w_full = (jnp.dot(w_hi, e_ref[...], preferred_element_type=F32)
                  + jnp.dot(w_lo, e_ref[...], preferred_element_type=F32))
        term = w_full * o_ref[...].astype(F32)
        o = term if o is None else o + term
    mix = jnp.dot(o.astype(BF16), wo_ref[...], preferred_element_type=F32)
    z = DN_ALPHA * x_ref[...] + mix
    out_ref[...] = _layer_norm(z, lng_ref[...], lnb_ref[...])


def _merge_out_proj(outs, lses, x2d, w_o, ln_g, ln_b, tm=512):
    m, d = x2d.shape
    lanes = 2 * HEAD_DIM
    head_of_col = jnp.arange(d) // HEAD_DIM
    expand = (jnp.arange(lanes)[:, None] == head_of_col[None, :]).astype(BF16)
    row = pl.BlockSpec((tm, d), lambda i: (i, 0))
    lrow = pl.BlockSpec((tm, lanes), lambda i: (i, 0))
    return pl.pallas_call(
        _merge_kernel,
        grid=(m // tm,),
        in_specs=[row, row, row, lrow, lrow, lrow, row, _resident(expand.shape),
                  _resident(w_o.shape), _resident(ln_g.shape), _resident(ln_b.shape)],
        out_specs=row,
        out_shape=jax.ShapeDtypeStruct((m, d), F32),
        compiler_params=_params(1),
        name="merge_out_proj",
    )(*outs, *lses, x2d, expand, w_o, ln_g, ln_b)


def _to_classes(z, batch, seq, dil):
    width = z.shape[-1]
    z = z.reshape(batch, seq // dil, dil, width).transpose(0, 2, 1, 3)
    return z.reshape(batch * dil, seq // dil, width)


def _from_classes(z, batch, seq, dil):
    width = z.shape[-1]
    z = z.reshape(batch, dil, seq // dil, width).transpose(0, 2, 1, 3)
    return z.reshape(batch * seq, width)


def kernel(x, ln_g, ln_b, rg_w_in, rg_conv_w, rg_conv_b, rg_w_a, rg_b_a, rg_w_x, rg_b_x,
           rg_lam, rg_w_out, at_w_qkv, at_w_o, ff_w_up, ff_conv_w, ff_conv_b, ff_w_down):
    batch, seq, d = x.shape
    m = batch * seq
    d_rnn = rg_w_out.shape[1]
    d_attn = at_w_o.shape[1]

    def vec(p):
        return p.reshape(1, -1)

    ct, kw = MXU_TILE, 2 * MXU_TILE
    k_starts = _gate_k_starts(d_rnn, LRU_BLOCKS, ct, kw)
    gate, u = _in_proj(x.reshape(m, d), rg_w_in[0].astype(BF16))
    gate = gate.reshape(batch, seq, d_rnn)
    u = u.reshape(batch, seq, d_rnn)
    conv_w, conv_b = rg_conv_w[0], vec(rg_conv_b[0])
    wg_f = _gate_weights(rg_w_a[0, 0], rg_w_x[0, 0], k_starts, kw, ct)
    wg_b = _gate_weights(rg_w_a[0, 1], rg_w_x[0, 1], k_starts, kw, ct)
    h_fwd = _lru_fwd(u, conv_w, conv_b, wg_f, vec(rg_b_a[0, 0]), vec(rg_b_x[0, 0]),
                     vec(rg_lam[0, 0]), k_starts, ct=ct, kw=kw)
    x1 = _lru_bwd(u, conv_w, conv_b, wg_b, vec(rg_b_a[0, 1]), vec(rg_b_x[0, 1]),
                  vec(rg_lam[0, 1]), h_fwd, gate, x, rg_w_out[0].astype(BF16),
                  vec(ln_g[0, 0]), vec(ln_b[0, 0]), k_starts, ct=ct, kw=kw)
    x2 = _conv_ffn(x1.reshape(m, d), seq, ff_w_up[0].astype(BF16), ff_conv_w[0],
                   vec(ff_conv_b[0]), ff_w_down[0].astype(BF16), vec(ln_g[0, 1]), vec(ln_b[0, 1]))

    qkv = _qkv_proj(x2, at_w_qkv[0].astype(BF16))
    qkv = qkv.reshape(batch, seq, len(ATTN_GROUPS), 3, d_attn)
    outs, lses = [], []
    for gi, (window, dil) in enumerate(ATTN_GROUPS):
        half = window // (2 * dil)
        q, k, v = (_to_classes(qkv[:, :, gi, t], batch, seq, dil) for t in range(3))
        o, lse = _band_attention(q, k, v, dil, half)
        outs.append(_from_classes(o, batch, seq, dil))
        lses.append(_from_classes(lse, batch, seq, dil))
    x3 = _merge_out_proj(outs, lses, x2, at_w_o[0].astype(BF16), vec(ln_g[1, 0]), vec(ln_b[1, 0]))
    x4 = _conv_ffn(x3, seq, ff_w_up[1].astype(BF16), ff_conv_w[1], vec(ff_conv_b[1]),
                   ff_w_down[1].astype(BF16), vec(ln_g[1, 1]), vec(ln_b[1, 1]))
    return x4.reshape(batch, seq, d)
```

```python
import functools
import math

import jax
import jax.numpy as jnp
from jax import lax
from jax.experimental import pallas as pl
from jax.experimental.pallas import tpu as pltpu

F32 = jnp.float32
BF16 = jnp.bfloat16

LRU_C = 8.0
LRU_BLOCKS = 16
ATTN_GROUPS = ((128, 1), (512, 4), (2048, 16))
N_HEADS = 16
HEAD_DIM = 64
LN_EPS = 1e-5
DEPTH = 2
DN_ALPHA = (2 * DEPTH) ** 0.25
NEG = -1e30

VMEM_LIMIT_BYTES = 56 * 1024 * 1024
MXU_TILE = 256
LANES = 128
BF16_ROWS = 16
F32_ROWS = 8

_GELU_C = math.sqrt(2.0 / math.pi)


def _gelu(x):
    return 0.5 * x * (1.0 + jnp.tanh(_GELU_C * (x + 0.044715 * (x * x * x))))


def _sigmoid(x):
    return 0.5 * (1.0 + jnp.tanh(0.5 * x))


def _layer_norm(z, g, b):
    mu = jnp.mean(z, axis=-1, keepdims=True)
    zc = z - mu
    var = jnp.mean(zc * zc, axis=-1, keepdims=True)
    return zc * lax.rsqrt(var + LN_EPS) * g + b


def _resident(shape):
    nd = len(shape)
    return pl.BlockSpec(shape, lambda *_: (0,) * nd, pipeline_mode=pl.Buffered(1))


def _params(n_axes):
    return pltpu.CompilerParams(
        dimension_semantics=("arbitrary",) * n_axes,
        vmem_limit_bytes=VMEM_LIMIT_BYTES)


def _in_proj_kernel(x_ref, w_ref, gate_ref, u_ref, xb_ref, *, d_rnn, cn):
    xb_ref[...] = x_ref[...].astype(BF16)
    for j in range(d_rnn // cn):
        g = jnp.dot(xb_ref[...], w_ref[:, j * cn:(j + 1) * cn], preferred_element_type=F32)
        gate_ref[:, j * cn:(j + 1) * cn] = _gelu(g).astype(BF16)
    for j in range(d_rnn // cn):
        u = jnp.dot(xb_ref[...], w_ref[:, d_rnn + j * cn:d_rnn + (j + 1) * cn],
                    preferred_element_type=F32)
        u_ref[:, j * cn:(j + 1) * cn] = u.astype(BF16)


def _in_proj(x2d, w_in, tm=512, cn=256):
    m, d = x2d.shape
    d_rnn = w_in.shape[1] // 2
    return pl.pallas_call(
        functools.partial(_in_proj_kernel, d_rnn=d_rnn, cn=cn),
        grid=(m // tm,),
        in_specs=[pl.BlockSpec((tm, d), lambda i: (i, 0)), _resident(w_in.shape)],
        out_specs=[pl.BlockSpec((tm, d_rnn), lambda i: (i, 0)),
                   pl.BlockSpec((tm, d_rnn), lambda i: (i, 0))],
        out_shape=[jax.ShapeDtypeStruct((m, d_rnn), BF16),
                   jax.ShapeDtypeStruct((m, d_rnn), BF16)],
        scratch_shapes=[pltpu.VMEM((tm, d), BF16)],
        compiler_params=_params(1),
        name="in_proj",
    )(x2d, w_in)


def _gate_k_starts(d_rnn, n_blocks, ct, kw):
    bw = d_rnn // n_blocks
    starts = []
    for j in range(d_rnn // ct):
        lo = (j * ct) // bw * bw
        hi = ((j * ct + ct - 1) // bw + 1) * bw
        k0 = min(lo // 128 * 128, d_rnn - kw)
        assert k0 <= lo and hi <= k0 + kw
        starts.append(k0)
    return tuple(starts)


def _scan_tile(a, b, h0, reverse):
    tt, cw = a.shape
    groups = tt // F32_ROWS
    a3 = a.reshape(groups, F32_ROWS, cw)
    b3 = b.reshape(groups, F32_ROWS, cw)
    row = lax.broadcasted_iota(jnp.int32, a3.shape, 1)
    for k in (1, 2, 4):
        if reverse:
            shift, keep = F32_ROWS - k, row < F32_ROWS - k
        else:
            shift, keep = k, row >= k
        ar = pltpu.roll(a3, shift, 1)
        br = pltpu.roll(b3, shift, 1)
        b3 = b3 + a3 * jnp.where(keep, br, 0.0)
        a3 = a3 * jnp.where(keep, ar, 1.0)
    out = [None] * groups
    h = h0
    for g in (reversed(range(groups)) if reverse else range(groups)):
        hg = b3[g] + a3[g] * h
        out[g] = hg
        h = hg[0:1] if reverse else hg[F32_ROWS - 1:F32_ROWS]
    return jnp.concatenate(out, axis=0), h


def _lru_tile(u_prev_ref, u_cur_ref, u_next_ref, cw_ref, cb_ref, wg_ref, ba_ref, bx_ref,
              lam_ref, carry_ref, emit, *, reverse, n_tiles, k_starts, kw, ct):
    step = pl.program_id(1)
    ti = n_tiles - 1 - step if reverse else step

    @pl.when(step == 0)
    def _():
        carry_ref[...] = jnp.zeros_like(carry_ref)

    tt = u_cur_ref.shape[0]
    halo = u_prev_ref.shape[0]
    prev = u_prev_ref[...].astype(F32) * (ti > 0).astype(F32)
    nxt = u_next_ref[...].astype(F32) * (ti < n_tiles - 1).astype(F32)
    ext = jnp.concatenate([prev, u_cur_ref[...].astype(F32), nxt], axis=0)
    n = tt + 2 * halo
    conv = (pltpu.roll(ext, 2, 0) * cw_ref[0:1, :] + pltpu.roll(ext, 1, 0) * cw_ref[1:2, :]
            + ext * cw_ref[2:3, :] + pltpu.roll(ext, n - 1, 0) * cw_ref[3:4, :]
            + cb_ref[...])[halo:halo + tt]
    ub = conv.astype(BF16)

    lam = lam_ref[...]
    neg_lam = -lam
    softplus = jnp.maximum(neg_lam, 0.0) + jnp.log1p(jnp.exp(-jnp.abs(neg_lam)))
    decay = -LRU_C * softplus

    for j, k0 in enumerate(k_starts):
        cols = slice(j * ct, (j + 1) * ct)
        g = jnp.dot(ub[:, k0:k0 + kw], wg_ref[j], preferred_element_type=F32)
        r = _sigmoid(g[:, :ct] + ba_ref[:, cols])
        i = _sigmoid(g[:, ct:] + bx_ref[:, cols])
        log_a = decay[:, cols] * r
        a = jnp.exp(log_a)
        gain = jnp.sqrt(-jnp.tanh(log_a) * (a * a + 1.0))
        b = gain * (i * conv[:, cols])
        h, h_last = _scan_tile(a, b, carry_ref[0:1, cols], reverse)
        carry_ref[0:1, cols] = h_last
        emit(j, cols, h)


def _lru_fwd_kernel(u_prev_ref, u_cur_ref, u_next_ref, cw_ref, cb_ref, wg_ref, ba_ref, bx_ref,
                    lam_ref, h_ref, carry_ref, **kw):
    def emit(j, cols, h):
        h_ref[:, cols] = h.astype(BF16)

    _lru_tile(u_prev_ref, u_cur_ref, u_next_ref, cw_ref, cb_ref, wg_ref, ba_ref, bx_ref,
              lam_ref, carry_ref, emit, reverse=False, **kw)


def _lru_bwd_kernel(u_prev_ref, u_cur_ref, u_next_ref, cw_ref, cb_ref, wg_ref, ba_ref, bx_ref,
                    lam_ref, hf_ref, gate_ref, x_ref, wout_ref, lng_ref, lnb_ref,
                    out_ref, carry_ref, y_ref, **kw):
    def emit(j, cols, h):
        y = gate_ref[:, cols].astype(F32) * (hf_ref[:, cols].astype(F32) + h)
        y_ref[:, cols] = y.astype(BF16)

    _lru_tile(u_prev_ref, u_cur_ref, u_next_ref, cw_ref, cb_ref, wg_ref, ba_ref, bx_ref,
              lam_ref, carry_ref, emit, reverse=True, **kw)
    mix = jnp.dot(y_ref[...], wout_ref[...], preferred_element_type=F32)
    z = DN_ALPHA * x_ref[...] + mix
    out_ref[...] = _layer_norm(z, lng_ref[...], lnb_ref[...])


def _lru_specs(batch, seq, d_rnn, tt, reverse):
    n_tiles = seq // tt
    halo = BF16_ROWS
    hb = tt // halo
    n_halo = seq // halo

    def t_of(i):
        return n_tiles - 1 - i if reverse else i

    cur = pl.BlockSpec((None, tt, d_rnn), lambda b, i: (b, t_of(i), 0))
    prev = pl.BlockSpec((None, halo, d_rnn),
                        lambda b, i: (b, jnp.maximum(t_of(i) * hb - 1, 0), 0))
    nxt = pl.BlockSpec((None, halo, d_rnn),
                       lambda b, i: (b, jnp.minimum((t_of(i) + 1) * hb, n_halo - 1), 0))
    return n_tiles, cur, prev, nxt


def _lru_fwd(u, conv_w, conv_b, wg, b_a, b_x, lam, k_starts, tt=256, ct=256, kw=512):
    batch, seq, d_rnn = u.shape
    n_tiles, cur, prev, nxt = _lru_specs(batch, seq, d_rnn, tt, False)
    kern = functools.partial(_lru_fwd_kernel, n_tiles=n_tiles, k_starts=k_starts, kw=kw, ct=ct)
    return pl.pallas_call(
        kern,
        grid=(batch, n_tiles),
        in_specs=[prev, cur, nxt, _resident(conv_w.shape), _resident(conv_b.shape),
                  _resident(wg.shape), _resident(b_a.shape), _resident(b_x.shape),
                  _resident(lam.shape)],
        out_specs=cur,
        out_shape=jax.ShapeDtypeStruct((batch, seq, d_rnn), BF16),
        scratch_shapes=[pltpu.VMEM((F32_ROWS, d_rnn), F32)],
        compiler_params=_params(2),
        name="lru_fwd",
    )(u, u, u, conv_w, conv_b, wg, b_a, b_x, lam)


def _lru_bwd(u, conv_w, conv_b, wg, b_a, b_x, lam, h_fwd, gate, x, w_out, ln_g, ln_b,
             k_starts, tt=256, ct=256, kw=512):
    batch, seq, d_rnn = u.shape
    d = x.shape[-1]
    n_tiles, cur, prev, nxt = _lru_specs(batch, seq, d_rnn, tt, True)
    xspec = pl.BlockSpec((None, tt, d), lambda b, i: (b, n_tiles - 1 - i, 0))
    kern = functools.partial(_lru_bwd_kernel, n_tiles=n_tiles, k_starts=k_starts, kw=kw, ct=ct)
    return pl.pallas_call(
        kern,
        grid=(batch, n_tiles),
        in_specs=[prev, cur, nxt, _resident(conv_w.shape), _resident(conv_b.shape),
                  _resident(wg.shape), _resident(b_a.shape), _resident(b_x.shape),
                  _resident(lam.shape), cur, cur, xspec, _resident(w_out.shape),
                  _resident(ln_g.shape), _resident(ln_b.shape)],
        out_specs=xspec,
        out_shape=jax.ShapeDtypeStruct((batch, seq, d), F32),
        scratch_shapes=[pltpu.VMEM((F32_ROWS, d_rnn), F32), pltpu.VMEM((tt, d_rnn), BF16)],
        compiler_params=_params(2),
        name="lru_bwd",
    )(u, u, u, conv_w, conv_b, wg, b_a, b_x, lam, h_fwd, gate, x, w_out, ln_g, ln_b)


def _gate_weights(w_a, w_x, k_starts, kw, ct):
    nb, bw, _ = w_a.shape
    d_rnn = nb * bw
    eye = jnp.eye(nb, dtype=w_a.dtype)

    def dense(w):
        return jnp.einsum('ncd,nm->ncmd', w, eye).reshape(d_rnn, d_rnn)

    wa, wx = dense(w_a), dense(w_x)
    tiles = [jnp.concatenate([wa[k0:k0 + kw, j * ct:(j + 1) * ct],
                              wx[k0:k0 + kw, j * ct:(j + 1) * ct]], axis=1)
             for j, k0 in enumerate(k_starts)]
    return jnp.stack(tiles).astype(BF16)


def _ffn_kernel(x_prev_ref, x_ref, x_next_ref, wup_ref, cw_ref, cb_ref, wdn_ref, lng_ref,
                lnb_ref, out_ref, xe_ref, acc_ref, *, tiles_per_seq, d_ff, cf):
    i = pl.program_id(0)
    tm = x_ref.shape[0]
    halo = x_prev_ref.shape[0]
    first = (i % tiles_per_seq == 0)
    last = (i % tiles_per_seq == tiles_per_seq - 1)
    xe_ref[0:halo, :] = (x_prev_ref[...] * jnp.where(first, 0.0, 1.0)).astype(BF16)
    xe_ref[halo:halo + tm, :] = x_ref[...].astype(BF16)
    xe_ref[halo + tm:, :] = (x_next_ref[...] * jnp.where(last, 0.0, 1.0)).astype(BF16)
    n = tm + 2 * halo
    for c in range(d_ff // cf):
        cols = slice(c * cf, (c + 1) * cf)
        g = jnp.dot(xe_ref[...], wup_ref[:, d_ff + c * cf:d_ff + (c + 1) * cf],
                    preferred_element_type=F32)
        v = jnp.dot(xe_ref[halo:halo + tm, :], wup_ref[:, cols], preferred_element_type=F32)
        gc = (pltpu.roll(g, 1, 0) * cw_ref[0:1, cols] + g * cw_ref[1:2, cols]
              + pltpu.roll(g, n - 1, 0) * cw_ref[2:3, cols] + cb_ref[:, cols])[halo:halo + tm]
        hmid = (_gelu(gc) * v).astype(BF16)
        part = jnp.dot(hmid, wdn_ref[cols, :], preferred_element_type=F32)
        if c == 0:
            acc_ref[...] = part
        else:
            acc_ref[...] += part
    z = DN_ALPHA * x_ref[...] + acc_ref[...]
    out_ref[...] = _layer_norm(z, lng_ref[...], lnb_ref[...])


def _conv_ffn(x2d, seq, w_up, conv_w, conv_b, w_down, ln_g, ln_b, tm=512, cf=256):
    m, d = x2d.shape
    d_ff = w_down.shape[0]
    halo = BF16_ROWS
    hb = tm // halo
    n_halo = m // halo
    kern = functools.partial(_ffn_kernel, tiles_per_seq=seq // tm, d_ff=d_ff, cf=cf)
    return pl.pallas_call(
        kern,
        grid=(m // tm,),
        in_specs=[pl.BlockSpec((halo, d), lambda i: (jnp.maximum(i * hb - 1, 0), 0)),
                  pl.BlockSpec((tm, d), lambda i: (i, 0)),
                  pl.BlockSpec((halo, d), lambda i: (jnp.minimum((i + 1) * hb, n_halo - 1), 0)),
                  _resident(w_up.shape), _resident(conv_w.shape), _resident(conv_b.shape),
                  _resident(w_down.shape), _resident(ln_g.shape), _resident(ln_b.shape)],
        out_specs=pl.BlockSpec((tm, d), lambda i: (i, 0)),
        out_shape=jax.ShapeDtypeStruct((m, d), F32),
        scratch_shapes=[pltpu.VMEM((tm + 2 * halo, d), BF16), pltpu.VMEM((tm, d), F32)],
        compiler_params=_params(1),
        name="conv_ffn",
    )(x2d, x2d, x2d, w_up, conv_w, conv_b, w_down, ln_g, ln_b)


def _qkv_kernel(x_ref, w_ref, o0_ref, o1_ref, o2_ref, xb_ref, scr_ref, *, cn, dils, width):
    out_refs = (o0_ref, o1_ref, o2_ref)
    tm = x_ref.shape[0]
    n_slabs = cn // LANES
    xb_ref[...] = x_ref[...].astype(BF16)
    for j in range(w_ref.shape[1] // cn):
        c0 = j * cn
        g, t, off = c0 // (3 * width), (c0 % (3 * width)) // width, c0 % width
        res = jnp.dot(xb_ref[...], w_ref[:, c0:c0 + cn], preferred_element_type=F32)
        dil = dils[g]
        if dil == 1:
            out_refs[g][t, 0, :, off:off + cn] = res.astype(BF16)
            continue
        buf = j % 2
        for s in range(n_slabs):
            scr_ref[buf, s] = res[:, s * LANES:(s + 1) * LANES]
        for r in range(dil):
            for s in range(n_slabs):
                rows = scr_ref[buf, s, pl.ds(r, tm // dil, stride=dil), :]
                out_refs[g][t, r, :, off + s * LANES:off + (s + 1) * LANES] = rows.astype(BF16)


def _qkv_proj(x2d, w_qkv, batch, seq, dils, tm=512, cn=512):
    m, d = x2d.shape
    width = w_qkv.shape[1] // (3 * len(dils))
    tiles_per_seq = seq // tm
    out_specs = [pl.BlockSpec((3, dil, tm // dil, width),
                              lambda i: (0, i // tiles_per_seq, i % tiles_per_seq, 0))
                 for dil in dils]
    out_shape = [jax.ShapeDtypeStruct((3, batch * dil, seq // dil, width), BF16) for dil in dils]
    return pl.pallas_call(
        functools.partial(_qkv_kernel, cn=cn, dils=dils, width=width),
        grid=(m // tm,),
        in_specs=[pl.BlockSpec((tm, d), lambda i: (i, 0)), _resident(w_qkv.shape)],
        out_specs=out_specs,
        out_shape=out_shape,
        scratch_shapes=[pltpu.VMEM((tm, d), BF16),
                        pltpu.VMEM((2, cn // LANES, tm, LANES), F32)],
        compiler_params=_params(1),
        name="qkv_proj",
    )(x2d, w_qkv)


def _attn_kernel(q_ref, kp_ref, kc_ref, kn_ref, vp_ref, vc_ref, vn_ref, o_ref, lse_ref,
                 bias_ref, *, n_q, half, dil):
    qi = pl.program_id(1)
    tq = q_ref.shape[0]
    win = tq + 2 * half
    lanes = 2 * HEAD_DIM

    @pl.when(jnp.logical_and(pl.program_id(0) == 0, qi == 0))
    def _():
        row = lax.broadcasted_iota(jnp.int32, (tq, win), 0)
        col = lax.broadcasted_iota(jnp.int32, (tq, win), 1)
        rel = col - half - row
        dist = jnp.abs(rel)
        in_band = dist <= half
        distf = (dist * dil).astype(F32)
        for variant in range(3):
            ok = in_band
            if variant == 0:
                ok = jnp.logical_and(ok, col >= half)
            if variant == 2:
                ok = jnp.logical_and(ok, col < half + tq)
            for h in range(N_HEADS):
                slope = 2.0 ** (-8.0 * (h + 1) / N_HEADS)
                bias_ref[variant, h] = jnp.where(ok, -slope * distf, NEG)

    variant = jnp.where(qi == 0, 0, jnp.where(qi == n_q - 1, 2, 1))
    kwin = jnp.concatenate([kp_ref[...], kc_ref[...], kn_ref[...]], axis=0)
    vwin = jnp.concatenate([vp_ref[...], vc_ref[...], vn_ref[...]], axis=0)
    lane = lax.broadcasted_iota(jnp.int32, (1, lanes), 1)
    lse_lane = lax.broadcasted_iota(jnp.int32, (tq, lanes), 1)
    lse_blk = jnp.zeros((tq, lanes), F32)
    zero = jnp.zeros((), BF16)
    scale = HEAD_DIM ** -0.5
    for hp in range(N_HEADS // 2):
        cols = slice(hp * lanes, (hp + 1) * lanes)
        q2 = q_ref[:, cols] * jnp.asarray(scale, BF16)
        k2 = kwin[:, cols]
        v2 = vwin[:, cols]
        o_pair = jnp.zeros((tq, lanes), F32)
        for side in range(2):
            h = 2 * hp + side
            mine = (lane >= HEAD_DIM) if side else (lane < HEAD_DIM)
            qh = jnp.where(mine, q2, zero)
            vh = jnp.where(mine, v2, zero)
            s = lax.dot_general(qh, k2, (((1,), (1,)), ((), ())), preferred_element_type=F32)
            s = s + bias_ref[variant, h]
            m = jnp.max(s, axis=-1, keepdims=True)
            p = jnp.exp(s - m)
            den = jnp.sum(p, axis=-1, keepdims=True)
            o = jnp.dot(p.astype(BF16), vh, preferred_element_type=F32)
            o_pair = o_pair + o * (1.0 / den)
            lse_blk = jnp.where(lse_lane == h, m + jnp.log(den), lse_blk)
        o_ref[:, cols] = o_pair.astype(BF16)
    lse_ref[...] = lse_blk


def _band_attention(qkv, dil, half, tq=128):
    _, n_seq, length, width = qkv.shape
    n_q = length // tq
    hb = tq // half
    n_half = length // half
    assert n_q >= 2

    def cur(t):
        return pl.BlockSpec((None, None, tq, width), lambda s, i: (t, s, i, 0))

    def prev(t):
        return pl.BlockSpec((None, None, half, width),
                            lambda s, i: (t, s, jnp.maximum(i * hb - 1, 0), 0))

    def nxt(t):
        return pl.BlockSpec((None, None, half, width),
                            lambda s, i: (t, s, jnp.minimum((i + 1) * hb, n_half - 1), 0))

    o_spec = pl.BlockSpec((None, tq, width), lambda s, i: (s, i, 0))
    lse_spec = pl.BlockSpec((None, tq, LANES), lambda s, i: (s, i, 0))
    kern = functools.partial(_attn_kernel, n_q=n_q, half=half, dil=dil)
    return pl.pallas_call(
        kern,
        grid=(n_seq, n_q),
        in_specs=[cur(0), prev(1), cur(1), nxt(1), prev(2), cur(2), nxt(2)],
        out_specs=[o_spec, lse_spec],
        out_shape=[jax.ShapeDtypeStruct((n_seq, length, width), BF16),
                   jax.ShapeDtypeStruct((n_seq, length, LANES), F32)],
        scratch_shapes=[pltpu.VMEM((3, N_HEADS, tq, tq + 2 * half), F32)],
        compiler_params=_params(2),
        name=f"band_attn_d{dil}",
    )(qkv, qkv, qkv, qkv, qkv, qkv, qkv)


def _merge_kernel(o0_ref, o1_ref, o2_ref, l0_ref, l1_ref, l2_ref, x_ref, e_ref, wo_ref,
                  lng_ref, lnb_ref, out_ref, lscr_ref, oscr_ref, acc_ref, *, dils):
    o_refs = (o0_ref, o1_ref, o2_ref)
    l_refs = (l0_ref, l1_ref, l2_ref)
    tm = x_ref.shape[0]
    n_slabs = x_ref.shape[1] // LANES
    lses = []
    for g, dil in enumerate(dils):
        if dil == 1:
            lses.append(l_refs[g][0])
            continue
        for r in range(dil):
            lscr_ref[g, pl.ds(r, tm // dil, stride=dil), :] = l_refs[g][r]
        lses.append(lscr_ref[g])
    top = jnp.maximum(jnp.maximum(lses[0], lses[1]), lses[2])
    ws = [jnp.exp(l - top) for l in lses]
    inv = 1.0 / (ws[0] + ws[1] + ws[2])
    n_strided = 0
    for g, dil in enumerate(dils):
        w = ws[g] * inv
        w_hi = w.astype(BF16)
        w_lo = (w - w_hi.astype(F32)).astype(BF16)
        w_full = (jnp.dot(w_hi, e_ref[...], preferred_element_type=F32)
                  + jnp.dot(w_lo, e_ref[...], preferred_element_type=F32))
        if dil == 1:
            term = w_full * o_refs[g][0].astype(F32)
            if g == 0:
                acc_ref[...] = term
            else:
                acc_ref[...] += term
            continue
        buf = n_strided
        n_strided += 1
        for r in range(dil):
            for s in range(n_slabs):
                oscr_ref[buf, s, pl.ds(r, tm // dil, stride=dil), :] = (
                    o_refs[g][r, :, s * LANES:(s + 1) * LANES].astype(F32))
        for s in range(n_slabs):
            cols = slice(s * LANES, (s + 1) * LANES)
            term = w_full[:, cols] * oscr_ref[buf, s]
            if g == 0:
                acc_ref[:, cols] = term
            else:
                acc_ref[:, cols] += term
    mix = jnp.dot(acc_ref[...].astype(BF16), wo_ref[...], preferred_element_type=F32)
    z = DN_ALPHA * x_ref[...] + mix
    out_ref[...] = _layer_norm(z, lng_ref[...], lnb_ref[...])


def _merge_out_proj(outs, lses, x2d, seq, dils, w_o, ln_g, ln_b, tm=512):
    m, d = x2d.shape
    tiles_per_seq = seq // tm
    head_of_col = jnp.arange(d) // HEAD_DIM
    expand = (jnp.arange(LANES)[:, None] == head_of_col[None, :]).astype(BF16)
    row = pl.BlockSpec((tm, d), lambda i: (i, 0))

    def classes(dil, width):
        return pl.BlockSpec((dil, tm // dil, width),
                            lambda i: (i // tiles_per_seq, i % tiles_per_seq, 0))

    n_strided = sum(dil != 1 for dil in dils)
    return pl.pallas_call(
        functools.partial(_merge_kernel, dils=dils),
        grid=(m // tm,),
        in_specs=([classes(dil, d) for dil in dils] + [classes(dil, LANES) for dil in dils]
                  + [row, _resident(expand.shape), _resident(w_o.shape),
                     _resident(ln_g.shape), _resident(ln_b.shape)]),
        out_specs=row,
        out_shape=jax.ShapeDtypeStruct((m, d), F32),
        scratch_shapes=[pltpu.VMEM((len(dils), tm, LANES), F32),
                        pltpu.VMEM((n_strided, d // LANES, tm, LANES), F32),
                        pltpu.VMEM((tm, d), F32)],
        compiler_params=_params(1),
        name="merge_out_proj",
    )(*outs, *lses, x2d, expand, w_o, ln_g, ln_b)


def kernel(x, ln_g, ln_b, rg_w_in, rg_conv_w, rg_conv_b, rg_w_a, rg_b_a, rg_w_x, rg_b_x,
           rg_lam, rg_w_out, at_w_qkv, at_w_o, ff_w_up, ff_conv_w, ff_conv_b, ff_w_down):
    batch, seq, d = x.shape
    m = batch * seq
    d_rnn = rg_w_out.shape[1]
    d_attn = at_w_o.shape[1]

    def vec(p):
        return p.reshape(1, -1)

    ct, kw = MXU_TILE, 2 * MXU_TILE
    k_starts = _gate_k_starts(d_rnn, LRU_BLOCKS, ct, kw)
    gate, u = _in_proj(x.reshape(m, d), rg_w_in[0].astype(BF16))
    gate = gate.reshape(batch, seq, d_rnn)
    u = u.reshape(batch, seq, d_rnn)
    conv_w, conv_b = rg_conv_w[0], vec(rg_conv_b[0])
    wg_f = _gate_weights(rg_w_a[0, 0], rg_w_x[0, 0], k_starts, kw, ct)
    wg_b = _gate_weights(rg_w_a[0, 1], rg_w_x[0, 1], k_starts, kw, ct)
    h_fwd = _lru_fwd(u, conv_w, conv_b, wg_f, vec(rg_b_a[0, 0]), vec(rg_b_x[0, 0]),
                     vec(rg_lam[0, 0]), k_starts, ct=ct, kw=kw)
    x1 = _lru_bwd(u, conv_w, conv_b, wg_b, vec(rg_b_a[0, 1]), vec(rg_b_x[0, 1]),
                  vec(rg_lam[0, 1]), h_fwd, gate, x, rg_w_out[0].astype(BF16),
                  vec(ln_g[0, 0]), vec(ln_b[0, 0]), k_starts, ct=ct, kw=kw)
    x2 = _conv_ffn(x1.reshape(m, d), seq, ff_w_up[0].astype(BF16), ff_conv_w[0],
                   vec(ff_conv_b[0]), ff_w_down[0].astype(BF16), vec(ln_g[0, 1]), vec(ln_b[0, 1]))

    dils = tuple(dil for _, dil in ATTN_GROUPS)
    qkv_groups = _qkv_proj(x2, at_w_qkv[0].astype(BF16), batch, seq, dils)
    outs, lses = [], []
    for qkv_g, (window, dil) in zip(qkv_groups, ATTN_GROUPS):
        o, lse = _band_attention(qkv_g, dil, window // (2 * dil))
        outs.append(o)
        lses.append(lse)
    x3 = _merge_out_proj(outs, lses, x2, seq, dils, at_w_o[0].astype(BF16),
                         vec(ln_g[1, 0]), vec(ln_b[1, 0]))
    x4 = _conv_ffn(x3, seq, ff_w_up[1].astype(BF16), ff_conv_w[1], vec(ff_conv_b[1]),
                   ff_w_down[1].astype(BF16), vec(ln_g[1, 1]), vec(ln_b[1, 1]))
    return x4.reshape(batch, seq, d)
```

```python
import functools
import math

import jax
import jax.numpy as jnp
from jax import lax
from jax.experimental import pallas as pl
from jax.experimental.pallas import tpu as pltpu

F32 = jnp.float32
BF16 = jnp.bfloat16

LRU_C = 8.0
LRU_BLOCKS = 16
ATTN_GROUPS = ((128, 1), (512, 4), (2048, 16))
N_HEADS = 16
HEAD_DIM = 64
LN_EPS = 1e-5
DEPTH = 2
DN_ALPHA = (2 * DEPTH) ** 0.25
NEG = -1e30

VMEM_LIMIT_BYTES = 56 * 1024 * 1024
MXU_TILE = 256
LANES = 128
BF16_ROWS = 16
F32_ROWS = 8

_GELU_C = math.sqrt(2.0 / math.pi)


def _gelu(x):
    return 0.5 * x * (1.0 + jnp.tanh(_GELU_C * (x + 0.044715 * (x * x * x))))


def _sigmoid(x):
    return 0.5 * (1.0 + jnp.tanh(0.5 * x))


def _layer_norm(z, g, b):
    mu = jnp.mean(z, axis=-1, keepdims=True)
    zc = z - mu
    var = jnp.mean(zc * zc, axis=-1, keepdims=True)
    return zc * lax.rsqrt(var + LN_EPS) * g + b


def _resident(shape):
    nd = len(shape)
    return pl.BlockSpec(shape, lambda *_: (0,) * nd, pipeline_mode=pl.Buffered(1))


def _params(n_axes):
    return pltpu.CompilerParams(
        dimension_semantics=("arbitrary",) * n_axes,
        vmem_limit_bytes=VMEM_LIMIT_BYTES)


def _in_proj_kernel(x_ref, w_ref, gate_ref, u_ref, xb_ref, *, d_rnn, cn):
    xb_ref[...] = x_ref[...].astype(BF16)
    for j in range(d_rnn // cn):
        g = jnp.dot(xb_ref[...], w_ref[:, j * cn:(j + 1) * cn], preferred_element_type=F32)
        gate_ref[:, j * cn:(j + 1) * cn] = _gelu(g).astype(BF16)
    for j in range(d_rnn // cn):
        u = jnp.dot(xb_ref[...], w_ref[:, d_rnn + j * cn:d_rnn + (j + 1) * cn],
                    preferred_element_type=F32)
        u_ref[:, j * cn:(j + 1) * cn] = u.astype(BF16)


def _in_proj(x2d, w_in, tm=512, cn=256):
    m, d = x2d.shape
    d_rnn = w_in.shape[1] // 2
    return pl.pallas_call(
        functools.partial(_in_proj_kernel, d_rnn=d_rnn, cn=cn),
        grid=(m // tm,),
        in_specs=[pl.BlockSpec((tm, d), lambda i: (i, 0)), _resident(w_in.shape)],
        out_specs=[pl.BlockSpec((tm, d_rnn), lambda i: (i, 0)),
                   pl.BlockSpec((tm, d_rnn), lambda i: (i, 0))],
        out_shape=[jax.ShapeDtypeStruct((m, d_rnn), BF16),
                   jax.ShapeDtypeStruct((m, d_rnn), BF16)],
        scratch_shapes=[pltpu.VMEM((tm, d), BF16)],
        compiler_params=_params(1),
        name="in_proj",
    )(x2d, w_in)


def _gate_k_starts(d_rnn, n_blocks, ct, kw):
    bw = d_rnn // n_blocks
    starts = []
    for j in range(d_rnn // ct):
        lo = (j * ct) // bw * bw
        hi = ((j * ct + ct - 1) // bw + 1) * bw
        k0 = min(lo // 128 * 128, d_rnn - kw)
        assert k0 <= lo and hi <= k0 + kw
        starts.append(k0)
    return tuple(starts)


def _scan_tile(a, b, h0, reverse):
    tt, cw = a.shape
    groups = tt // F32_ROWS
    a3 = a.reshape(groups, F32_ROWS, cw)
    b3 = b.reshape(groups, F32_ROWS, cw)
    row = lax.broadcasted_iota(jnp.int32, a3.shape, 1)
    for k in (1, 2, 4):
        if reverse:
            shift, keep = F32_ROWS - k, row < F32_ROWS - k
        else:
            shift, keep = k, row >= k
        ar = pltpu.roll(a3, shift, 1)
        br = pltpu.roll(b3, shift, 1)
        b3 = b3 + a3 * jnp.where(keep, br, 0.0)
        a3 = a3 * jnp.where(keep, ar, 1.0)
    out = [None] * groups
    h = h0
    for g in (reversed(range(groups)) if reverse else range(groups)):
        hg = b3[g] + a3[g] * h
        out[g] = hg
        h = hg[0:1] if reverse else hg[F32_ROWS - 1:F32_ROWS]
    return jnp.concatenate(out, axis=0), h


def _lru_tile(u_prev_ref, u_cur_ref, u_next_ref, cw_ref, cb_ref, wg_ref, ba_ref, bx_ref,
              lam_ref, carry_ref, emit, *, reverse, n_tiles, k_starts, kw, ct):
    step = pl.program_id(1)
    ti = n_tiles - 1 - step if reverse else step

    @pl.when(step == 0)
    def _():
        carry_ref[...] = jnp.zeros_like(carry_ref)

    tt = u_cur_ref.shape[0]
    halo = u_prev_ref.shape[0]
    prev = u_prev_ref[...].astype(F32) * (ti > 0).astype(F32)
    nxt = u_next_ref[...].astype(F32) * (ti < n_tiles - 1).astype(F32)
    ext = jnp.concatenate([prev, u_cur_ref[...].astype(F32), nxt], axis=0)
    n = tt + 2 * halo
    conv = (pltpu.roll(ext, 2, 0) * cw_ref[0:1, :] + pltpu.roll(ext, 1, 0) * cw_ref[1:2, :]
            + ext * cw_ref[2:3, :] + pltpu.roll(ext, n - 1, 0) * cw_ref[3:4, :]
            + cb_ref[...])[halo:halo + tt]
    ub = conv.astype(BF16)

    lam = lam_ref[...]
    neg_lam = -lam
    softplus = jnp.maximum(neg_lam, 0.0) + jnp.log1p(jnp.exp(-jnp.abs(neg_lam)))
    decay = -LRU_C * softplus

    for j, k0 in enumerate(k_starts):
        cols = slice(j * ct, (j + 1) * ct)
        g = jnp.dot(ub[:, k0:k0 + kw], wg_ref[j], preferred_element_type=F32)
        r = _sigmoid(g[:, :ct] + ba_ref[:, cols])
        i = _sigmoid(g[:, ct:] + bx_ref[:, cols])
        log_a = decay[:, cols] * r
        a = jnp.exp(log_a)
        gain = jnp.sqrt(-jnp.tanh(log_a) * (a * a + 1.0))
        b = gain * (i * conv[:, cols])
        h, h_last = _scan_tile(a, b, carry_ref[0:1, cols], reverse)
        carry_ref[0:1, cols] = h_last
        emit(j, cols, h)


def _lru_fwd_kernel(u_prev_ref, u_cur_ref, u_next_ref, cw_ref, cb_ref, wg_ref, ba_ref, bx_ref,
                    lam_ref, h_ref, carry_ref, **kw):
    def emit(j, cols, h):
        h_ref[:, cols] = h.astype(BF16)

    _lru_tile(u_prev_ref, u_cur_ref, u_next_ref, cw_ref, cb_ref, wg_ref, ba_ref, bx_ref,
              lam_ref, carry_ref, emit, reverse=False, **kw)


def _lru_bwd_kernel(u_prev_ref, u_cur_ref, u_next_ref, cw_ref, cb_ref, wg_ref, ba_ref, bx_ref,
                    lam_ref, hf_ref, gate_ref, x_ref, wout_ref, lng_ref, lnb_ref,
                    out_ref, carry_ref, y_ref, **kw):
    def emit(j, cols, h):
        y = gate_ref[:, cols].astype(F32) * (hf_ref[:, cols].astype(F32) + h)
        y_ref[:, cols] = y.astype(BF16)

    _lru_tile(u_prev_ref, u_cur_ref, u_next_ref, cw_ref, cb_ref, wg_ref, ba_ref, bx_ref,
              lam_ref, carry_ref, emit, reverse=True, **kw)
    mix = jnp.dot(y_ref[...], wout_ref[...], preferred_element_type=F32)
    z = DN_ALPHA * x_ref[...] + mix
    out_ref[...] = _layer_norm(z, lng_ref[...], lnb_ref[...])


def _lru_specs(batch, seq, d_rnn, tt, reverse):
    n_tiles = seq // tt
    halo = BF16_ROWS
    hb = tt // halo
    n_halo = seq // halo

    def t_of(i):
        return n_tiles - 1 - i if reverse else i

    cur = pl.BlockSpec((None, tt, d_rnn), lambda b, i: (b, t_of(i), 0))
    prev = pl.BlockSpec((None, halo, d_rnn),
                        lambda b, i: (b, jnp.maximum(t_of(i) * hb - 1, 0), 0))
    nxt = pl.BlockSpec((None, halo, d_rnn),
                       lambda b, i: (b, jnp.minimum((t_of(i) + 1) * hb, n_halo - 1), 0))
    return n_tiles, cur, prev, nxt


def _lru_fwd(u, conv_w, conv_b, wg, b_a, b_x, lam, k_starts, tt=256, ct=256, kw=512):
    batch, seq, d_rnn = u.shape
    n_tiles, cur, prev, nxt = _lru_specs(batch, seq, d_rnn, tt, False)
    kern = functools.partial(_lru_fwd_kernel, n_tiles=n_tiles, k_starts=k_starts, kw=kw, ct=ct)
    return pl.pallas_call(
        kern,
        grid=(batch, n_tiles),
        in_specs=[prev, cur, nxt, _resident(conv_w.shape), _resident(conv_b.shape),
                  _resident(wg.shape), _resident(b_a.shape), _resident(b_x.shape),
                  _resident(lam.shape)],
        out_specs=cur,
        out_shape=jax.ShapeDtypeStruct((batch, seq, d_rnn), BF16),
        scratch_shapes=[pltpu.VMEM((F32_ROWS, d_rnn), F32)],
        compiler_params=_params(2),
        name="lru_fwd",
    )(u, u, u, conv_w, conv_b, wg, b_a, b_x, lam)


def _lru_bwd(u, conv_w, conv_b, wg, b_a, b_x, lam, h_fwd, gate, x, w_out, ln_g, ln_b,
             k_starts, tt=256, ct=256, kw=512):
    batch, seq, d_rnn = u.shape
    d = x.shape[-1]
    n_tiles, cur, prev, nxt = _lru_specs(batch, seq, d_rnn, tt, True)
    xspec = pl.BlockSpec((None, tt, d), lambda b, i: (b, n_tiles - 1 - i, 0))
    kern = functools.partial(_lru_bwd_kernel, n_tiles=n_tiles, k_starts=k_starts, kw=kw, ct=ct)
    return pl.pallas_call(
        kern,
        grid=(batch, n_tiles),
        in_specs=[prev, cur, nxt, _resident(conv_w.shape), _resident(conv_b.shape),
                  _resident(wg.shape), _resident(b_a.shape), _resident(b_x.shape),
                  _resident(lam.shape), cur, cur, xspec, _resident(w_out.shape),
                  _resident(ln_g.shape), _resident(ln_b.shape)],
        out_specs=xspec,
        out_shape=jax.ShapeDtypeStruct((batch, seq, d), F32),
        scratch_shapes=[pltpu.VMEM((F32_ROWS, d_rnn), F32), pltpu.VMEM((tt, d_rnn), BF16)],
        compiler_params=_params(2),
        name="lru_bwd",
    )(u, u, u, conv_w, conv_b, wg, b_a, b_x, lam, h_fwd, gate, x, w_out, ln_g, ln_b)


def _gate_weights(w_a, w_x, k_starts, kw, ct):
    nb, bw, _ = w_a.shape
    d_rnn = nb * bw
    eye = jnp.eye(nb, dtype=w_a.dtype)

    def dense(w):
        return jnp.einsum('ncd,nm->ncmd', w, eye).reshape(d_rnn, d_rnn)

    wa, wx = dense(w_a), dense(w_x)
    tiles = [jnp.concatenate([wa[k0:k0 + kw, j * ct:(j + 1) * ct],
                              wx[k0:k0 + kw, j * ct:(j + 1) * ct]], axis=1)
             for j, k0 in enumerate(k_starts)]
    return jnp.stack(tiles).astype(BF16)


def _ffn_kernel(x_prev_ref, x_ref, x_next_ref, wup_ref, cw_ref, cb_ref, wdn_ref, lng_ref,
                lnb_ref, out_ref, xe_ref, acc_ref, *, tiles_per_seq, d_ff, cf):
    i = pl.program_id(0)
    tm = x_ref.shape[0]
    halo = x_prev_ref.shape[0]
    first = (i % tiles_per_seq == 0)
    last = (i % tiles_per_seq == tiles_per_seq - 1)
    xe_ref[0:halo, :] = (x_prev_ref[...] * jnp.where(first, 0.0, 1.0)).astype(BF16)
    xe_ref[halo:halo + tm, :] = x_ref[...].astype(BF16)
    xe_ref[halo + tm:, :] = (x_next_ref[...] * jnp.where(last, 0.0, 1.0)).astype(BF16)
    n = tm + 2 * halo
    for c in range(d_ff // cf):
        cols = slice(c * cf, (c + 1) * cf)
        g = jnp.dot(xe_ref[...], wup_ref[:, d_ff + c * cf:d_ff + (c + 1) * cf],
                    preferred_element_type=F32)
        v = jnp.dot(xe_ref[halo:halo + tm, :], wup_ref[:, cols], preferred_element_type=F32)
        gc = (pltpu.roll(g, 1, 0) * cw_ref[0:1, cols] + g * cw_ref[1:2, cols]
              + pltpu.roll(g, n - 1, 0) * cw_ref[2:3, cols] + cb_ref[:, cols])[halo:halo + tm]
        hmid = (_gelu(gc) * v).astype(BF16)
        part = jnp.dot(hmid, wdn_ref[cols, :], preferred_element_type=F32)
        if c == 0:
            acc_ref[...] = part
        else:
            acc_ref[...] += part
    z = DN_ALPHA * x_ref[...] + acc_ref[...]
    out_ref[...] = _layer_norm(z, lng_ref[...], lnb_ref[...])


def _conv_ffn(x2d, seq, w_up, conv_w, conv_b, w_down, ln_g, ln_b, tm=1024, cf=256):
    m, d = x2d.shape
    d_ff = w_down.shape[0]
    halo = BF16_ROWS
    hb = tm // halo
    n_halo = m // halo
    kern = functools.partial(_ffn_kernel, tiles_per_seq=seq // tm, d_ff=d_ff, cf=cf)
    return pl.pallas_call(
        kern,
        grid=(m // tm,),
        in_specs=[pl.BlockSpec((halo, d), lambda i: (jnp.maximum(i * hb - 1, 0), 0)),
                  pl.BlockSpec((tm, d), lambda i: (i, 0)),
                  pl.BlockSpec((halo, d), lambda i: (jnp.minimum((i + 1) * hb, n_halo - 1), 0)),
                  _resident(w_up.shape), _resident(conv_w.shape), _resident(conv_b.shape),
                  _resident(w_down.shape), _resident(ln_g.shape), _resident(ln_b.shape)],
        out_specs=pl.BlockSpec((tm, d), lambda i: (i, 0)),
        out_shape=jax.ShapeDtypeStruct((m, d), F32),
        scratch_shapes=[pltpu.VMEM((tm + 2 * halo, d), BF16), pltpu.VMEM((tm, d), F32)],
        compiler_params=_params(1),
        name="conv_ffn",
    )(x2d, x2d, x2d, w_up, conv_w, conv_b, w_down, ln_g, ln_b)


def _class_permutation(tm, dil):
    out_row = jnp.arange(tm)
    src = (out_row % (tm // dil)) * dil + out_row // (tm // dil)
    return (src[:, None] == jnp.arange(tm)[None, :]).astype(BF16)


def _qkv_kernel(x_ref, w_ref, perm_ref, o0_ref, o1_ref, o2_ref, xb_ref, *, cn, dils, strided,
                width):
    out_refs = (o0_ref, o1_ref, o2_ref)
    tm = x_ref.shape[0]
    xb_ref[0] = x_ref[...].astype(BF16)
    lhs_of = {1: 0}
    for slot, dil in enumerate(strided, start=1):
        xb_ref[slot] = jnp.dot(perm_ref[slot - 1], xb_ref[0],
                               preferred_element_type=F32).astype(BF16)
        lhs_of[dil] = slot
    for j in range(w_ref.shape[1] // cn):
        c0 = j * cn
        g, t, off = c0 // (3 * width), (c0 % (3 * width)) // width, c0 % width
        dil = dils[g]
        res = jnp.dot(xb_ref[lhs_of[dil]], w_ref[:, c0:c0 + cn], preferred_element_type=F32)
        rows = tm // dil
        for r in range(dil):
            out_refs[g][t, r, :, off:off + cn] = res[r * rows:(r + 1) * rows].astype(BF16)


def _qkv_proj(x2d, w_qkv, batch, seq, dils, tm=512, cn=512):
    m, d = x2d.shape
    width = w_qkv.shape[1] // (3 * len(dils))
    tiles_per_seq = seq // tm
    strided = tuple(sorted({dil for dil in dils if dil != 1}))
    perms = jnp.stack([_class_permutation(tm, dil) for dil in strided])
    out_specs = [pl.BlockSpec((3, dil, tm // dil, width),
                              lambda i: (0, i // tiles_per_seq, i % tiles_per_seq, 0))
                 for dil in dils]
    out_shape = [jax.ShapeDtypeStruct((3, batch * dil, seq // dil, width), BF16) for dil in dils]
    return pl.pallas_call(
        functools.partial(_qkv_kernel, cn=cn, dils=dils, strided=strided, width=width),
        grid=(m // tm,),
        in_specs=[pl.BlockSpec((tm, d), lambda i: (i, 0)), _resident(w_qkv.shape),
                  _resident(perms.shape)],
        out_specs=out_specs,
        out_shape=out_shape,
        scratch_shapes=[pltpu.VMEM((1 + len(strided), tm, d), BF16)],
        compiler_params=_params(1),
        name="qkv_proj",
    )(x2d, w_qkv, perms)


def _attn_kernel(q_ref, kp_ref, kc_ref, kn_ref, vp_ref, vc_ref, vn_ref, o_ref, lse_ref,
                 bias_ref, *, n_q, half, dil):
    qi = pl.program_id(1)
    tq = q_ref.shape[0]
    win = tq + 2 * half
    lanes = 2 * HEAD_DIM

    @pl.when(jnp.logical_and(pl.program_id(0) == 0, qi == 0))
    def _():
        row = lax.broadcasted_iota(jnp.int32, (tq, win), 0)
        col = lax.broadcasted_iota(jnp.int32, (tq, win), 1)
        rel = col - half - row
        dist = jnp.abs(rel)
        in_band = dist <= half
        distf = (dist * dil).astype(F32)
        for variant in range(3):
            ok = in_band
            if variant == 0:
                ok = jnp.logical_and(ok, col >= half)
            if variant == 2:
                ok = jnp.logical_and(ok, col < half + tq)
            for h in range(N_HEADS):
                slope = 2.0 ** (-8.0 * (h + 1) / N_HEADS)
                rows = slice((h % 2) * tq, (h % 2 + 1) * tq)
                bias_ref[variant, h // 2, rows, :] = jnp.where(ok, -slope * distf, NEG)

    variant = jnp.where(qi == 0, 0, jnp.where(qi == n_q - 1, 2, 1))
    kwin = jnp.concatenate([kp_ref[...], kc_ref[...], kn_ref[...]], axis=0)
    vwin = jnp.concatenate([vp_ref[...], vc_ref[...], vn_ref[...]], axis=0)
    lane = lax.broadcasted_iota(jnp.int32, (1, lanes), 1)
    first_head = lane < HEAD_DIM
    lse_lane = lax.broadcasted_iota(jnp.int32, (tq, lanes), 1)
    lse_blk = jnp.zeros((tq, lanes), F32)
    zero = jnp.zeros((), BF16)
    scale = HEAD_DIM ** -0.5
    for hp in range(N_HEADS // 2):
        cols = slice(hp * lanes, (hp + 1) * lanes)
        q2 = q_ref[:, cols] * jnp.asarray(scale, BF16)
        q_st = jnp.concatenate([jnp.where(first_head, q2, zero),
                                jnp.where(first_head, zero, q2)], axis=0)
        s = lax.dot_general(q_st, kwin[:, cols], (((1,), (1,)), ((), ())),
                            preferred_element_type=F32)
        s = s + bias_ref[variant, hp]
        m = jnp.max(s, axis=-1, keepdims=True)
        p = jnp.exp(s - m)
        den = jnp.sum(p, axis=-1, keepdims=True)
        o_st = jnp.dot(p.astype(BF16), vwin[:, cols], preferred_element_type=F32)
        o_st = o_st * (1.0 / den)
        o_ref[:, cols] = jnp.where(first_head, o_st[:tq], o_st[tq:]).astype(BF16)
        lse = m + jnp.log(den)
        lse_blk = jnp.where(lse_lane == 2 * hp, lse[:tq], lse_blk)
        lse_blk = jnp.where(lse_lane == 2 * hp + 1, lse[tq:], lse_blk)
    lse_ref[...] = lse_blk


def _band_attention(qkv, dil, half, tq=128):
    _, n_seq, length, width = qkv.shape
    n_q = length // tq
    hb = tq // half
    n_half = length // half
    assert n_q >= 2

    def cur(t):
        return pl.BlockSpec((None, None, tq, width), lambda s, i: (t, s, i, 0))

    def prev(t):
        return pl.BlockSpec((None, None, half, width),
                            lambda s, i: (t, s, jnp.maximum(i * hb - 1, 0), 0))

    def nxt(t):
        return pl.BlockSpec((None, None, half, width),
                            lambda s, i: (t, s, jnp.minimum((i + 1) * hb, n_half - 1), 0))

    o_spec = pl.BlockSpec((None, tq, width), lambda s, i: (s, i, 0))
    lse_spec = pl.BlockSpec((None, tq, LANES), lambda s, i: (s, i, 0))
    kern = functools.partial(_attn_kernel, n_q=n_q, half=half, dil=dil)
    return pl.pallas_call(
        kern,
        grid=(n_seq, n_q),
        in_specs=[cur(0), prev(1), cur(1), nxt(1), prev(2), cur(2), nxt(2)],
        out_specs=[o_spec, lse_spec],
        out_shape=[jax.ShapeDtypeStruct((n_seq, length, width), BF16),
                   jax.ShapeDtypeStruct((n_seq, length, LANES), F32)],
        scratch_shapes=[pltpu.VMEM((3, N_HEADS // 2, 2 * tq, tq + 2 * half), F32)],
        compiler_params=_params(2),
        name=f"band_attn_d{dil}",
    )(qkv, qkv, qkv, qkv, qkv, qkv, qkv)


def _merge_kernel(o0_ref, o1_ref, o2_ref, l0_ref, l1_ref, l2_ref, x_ref, e_ref, wo_ref,
                  lng_ref, lnb_ref, unperm_ref, out_ref, lscr_ref, acc_ref, *, dils, strided):
    o_refs = (o0_ref, o1_ref, o2_ref)
    l_refs = (l0_ref, l1_ref, l2_ref)
    tm, d = x_ref.shape
    lses = []
    for g, dil in enumerate(dils):
        if dil == 1:
            lses.append(l_refs[g][0])
            continue
        for r in range(dil):
            lscr_ref[g, pl.ds(r, tm // dil, stride=dil), :] = l_refs[g][r]
        lses.append(lscr_ref[g])
    top = jnp.maximum(jnp.maximum(lses[0], lses[1]), lses[2])
    ws = [jnp.exp(l - top) for l in lses]
    inv = 1.0 / (ws[0] + ws[1] + ws[2])
    for g, dil in enumerate(dils):
        w = ws[g] * inv
        w_hi = w.astype(BF16)
        w_lo = (w - w_hi.astype(F32)).astype(BF16)
        w_full = (jnp.dot(w_hi, e_ref[...], preferred_element_type=F32)
                  + jnp.dot(w_lo, e_ref[...], preferred_element_type=F32))
        if dil == 1:
            o_tok = o_refs[g][0].astype(F32)
        else:
            o_tok = jnp.dot(unperm_ref[strided.index(dil)], o_refs[g][...].reshape(tm, d),
                            preferred_element_type=F32)
        if g == 0:
            acc_ref[...] = w_full * o_tok
        else:
            acc_ref[...] += w_full * o_tok
    mix = jnp.dot(acc_ref[...].astype(BF16), wo_ref[...], preferred_element_type=F32)
    z = DN_ALPHA * x_ref[...] + mix
    out_ref[...] = _layer_norm(z, lng_ref[...], lnb_ref[...])


def _merge_out_proj(outs, lses, x2d, seq, dils, w_o, ln_g, ln_b, tm=512):
    m, d = x2d.shape
    tiles_per_seq = seq // tm
    head_of_col = jnp.arange(d) // HEAD_DIM
    expand = (jnp.arange(LANES)[:, None] == head_of_col[None, :]).astype(BF16)
    row = pl.BlockSpec((tm, d), lambda i: (i, 0))

    def classes(dil, width):
        return pl.BlockSpec((dil, tm // dil, width),
                            lambda i: (i // tiles_per_seq, i % tiles_per_seq, 0))

    strided = tuple(sorted({dil for dil in dils if dil != 1}))
    unperms = jnp.stack([_class_permutation(tm, dil).T for dil in strided])
    return pl.pallas_call(
        functools.partial(_merge_kernel, dils=dils, strided=strided),
        grid=(m // tm,),
        in_specs=([classes(dil, d) for dil in dils] + [classes(dil, LANES) for dil in dils]
                  + [row, _resident(expand.shape), _resident(w_o.shape),
                     _resident(ln_g.shape), _resident(ln_b.shape), _resident(unperms.shape)]),
        out_specs=row,
        out_shape=jax.ShapeDtypeStruct((m, d), F32),
        scratch_shapes=[pltpu.VMEM((len(dils), tm, LANES), F32),
                        pltpu.VMEM((tm, d), F32)],
        compiler_params=_params(1),
        name="merge_out_proj",
    )(*outs, *lses, x2d, expand, w_o, ln_g, ln_b, unperms)


def kernel(x, ln_g, ln_b, rg_w_in, rg_conv_w, rg_conv_b, rg_w_a, rg_b_a, rg_w_x, rg_b_x,
           rg_lam, rg_w_out, at_w_qkv, at_w_o, ff_w_up, ff_conv_w, ff_conv_b, ff_w_down):
    batch, seq, d = x.shape
    m = batch * seq
    d_rnn = rg_w_out.shape[1]
    d_attn = at_w_o.shape[1]

    def vec(p):
        return p.reshape(1, -1)

    ct, kw = MXU_TILE, 2 * MXU_TILE
    k_starts = _gate_k_starts(d_rnn, LRU_BLOCKS, ct, kw)
    gate, u = _in_proj(x.reshape(m, d), rg_w_in[0].astype(BF16))
    gate = gate.reshape(batch, seq, d_rnn)
    u = u.reshape(batch, seq, d_rnn)
    conv_w, conv_b = rg_conv_w[0], vec(rg_conv_b[0])
    wg_f = _gate_weights(rg_w_a[0, 0], rg_w_x[0, 0], k_starts, kw, ct)
    wg_b = _gate_weights(rg_w_a[0, 1], rg_w_x[0, 1], k_starts, kw, ct)
    h_fwd = _lru_fwd(u, conv_w, conv_b, wg_f, vec(rg_b_a[0, 0]), vec(rg_b_x[0, 0]),
                     vec(rg_lam[0, 0]), k_starts, ct=ct, kw=kw)
    x1 = _lru_bwd(u, conv_w, conv_b, wg_b, vec(rg_b_a[0, 1]), vec(rg_b_x[0, 1]),
                  vec(rg_lam[0, 1]), h_fwd, gate, x, rg_w_out[0].astype(BF16),
                  vec(ln_g[0, 0]), vec(ln_b[0, 0]), k_starts, ct=ct, kw=kw)
    x2 = _conv_ffn(x1.reshape(m, d), seq, ff_w_up[0].astype(BF16), ff_conv_w[0],
                   vec(ff_conv_b[0]), ff_w_down[0].astype(BF16), vec(ln_g[0, 1]), vec(ln_b[0, 1]))

    dils = tuple(dil for _, dil in ATTN_GROUPS)
    qkv_groups = _qkv_proj(x2, at_w_qkv[0].astype(BF16), batch, seq, dils)
    outs, lses = [], []
    for qkv_g, (window, dil) in zip(qkv_groups, ATTN_GROUPS):
        o, lse = _band_attention(qkv_g, dil, window // (2 * dil))
        outs.append(o)
        lses.append(lse)
    x3 = _merge_out_proj(outs, lses, x2, seq, dils, at_w_o[0].astype(BF16),
                         vec(ln_g[1, 0]), vec(ln_b[1, 0]))
    x4 = _conv_ffn(x3, seq, ff_w_up[1].astype(BF16), ff_conv_w[1], vec(ff_conv_b[1]),
                   ff_w_down[1].astype(BF16), vec(ln_g[1, 1]), vec(ln_b[1, 1]))
    return x4.reshape(batch, seq, d)
```

```python
import functools
import math

import jax
import jax.numpy as jnp
from jax import lax
from jax.experimental import pallas as pl
from jax.experimental.pallas import tpu as pltpu

F32 = jnp.float32
BF16 = jnp.bfloat16

LRU_C = 8.0
LRU_BLOCKS = 16
ATTN_GROUPS = ((128, 1), (512, 4), (2048, 16))
N_HEADS = 16
HEAD_DIM = 64
LN_EPS = 1e-5
DEPTH = 2
DN_ALPHA = (2 * DEPTH) ** 0.25
NEG = -1e30

VMEM_LIMIT_BYTES = 56 * 1024 * 1024
MXU_TILE = 256
LANES = 128
BF16_ROWS = 16
F32_ROWS = 8

_GELU_C = math.sqrt(2.0 / math.pi)


def _gelu(x):
    inner = x * (_GELU_C + (_GELU_C * 0.044715) * (x * x))
    return x * (0.5 + 0.5 * jnp.tanh(inner))


def _sigmoid(x):
    return 0.5 * (1.0 + jnp.tanh(0.5 * x))


def _layer_norm(z, g, b):
    mu = jnp.mean(z, axis=-1, keepdims=True)
    zc = z - mu
    var = jnp.mean(zc * zc, axis=-1, keepdims=True)
    return zc * lax.rsqrt(var + LN_EPS) * g + b


def _resident(shape):
    nd = len(shape)
    return pl.BlockSpec(shape, lambda *_: (0,) * nd, pipeline_mode=pl.Buffered(1))


def _params(n_axes):
    return pltpu.CompilerParams(
        dimension_semantics=("arbitrary",) * n_axes,
        vmem_limit_bytes=VMEM_LIMIT_BYTES)


def _fill_with_halo(xe_ref, x_prev_ref, x_ref, x_next_ref, i, tiles_per_seq):
    tm = x_ref.shape[0]
    halo = x_prev_ref.shape[0]
    first = (i % tiles_per_seq == 0)
    last = (i % tiles_per_seq == tiles_per_seq - 1)
    xe_ref[0:halo, :] = (x_prev_ref[...] * jnp.where(first, 0.0, 1.0)).astype(BF16)
    xe_ref[halo:halo + tm, :] = x_ref[...].astype(BF16)
    xe_ref[halo + tm:, :] = (x_next_ref[...] * jnp.where(last, 0.0, 1.0)).astype(BF16)


def _in_proj_kernel(x_prev_ref, x_ref, x_next_ref, w_ref, cw_ref, cb_ref, gate_ref, u_ref,
                    xe_ref, *, tiles_per_seq, d_rnn, cn):
    tm = x_ref.shape[0]
    halo = x_prev_ref.shape[0]
    n = tm + 2 * halo
    _fill_with_halo(xe_ref, x_prev_ref, x_ref, x_next_ref, pl.program_id(0), tiles_per_seq)
    for j in range(d_rnn // cn):
        cols = slice(j * cn, (j + 1) * cn)
        g = jnp.dot(xe_ref[halo:halo + tm, :], w_ref[:, cols], preferred_element_type=F32)
        u = jnp.dot(xe_ref[...], w_ref[:, d_rnn + j * cn:d_rnn + (j + 1) * cn],
                    preferred_element_type=F32)
        gate_ref[:, cols] = _gelu(g).astype(BF16)
        conv = (pltpu.roll(u, 2, 0) * cw_ref[0:1, cols] + pltpu.roll(u, 1, 0) * cw_ref[1:2, cols]
                + u * cw_ref[2:3, cols] + pltpu.roll(u, n - 1, 0) * cw_ref[3:4, cols]
                + cb_ref[:, cols])
        u_ref[:, cols] = conv[halo:halo + tm].astype(BF16)


def _halo_specs(m, d, tm, halo):
    hb = tm // halo
    n_halo = m // halo
    return [pl.BlockSpec((halo, d), lambda i: (jnp.maximum(i * hb - 1, 0), 0)),
            pl.BlockSpec((tm, d), lambda i: (i, 0)),
            pl.BlockSpec((halo, d), lambda i: (jnp.minimum((i + 1) * hb, n_halo - 1), 0))]


def _in_proj(x2d, seq, w_in, conv_w, conv_b, tm=512, cn=256):
    m, d = x2d.shape
    d_rnn = w_in.shape[1] // 2
    halo = BF16_ROWS
    kern = functools.partial(_in_proj_kernel, tiles_per_seq=seq // tm, d_rnn=d_rnn, cn=cn)
    return pl.pallas_call(
        kern,
        grid=(m // tm,),
        in_specs=_halo_specs(m, d, tm, halo) + [_resident(w_in.shape), _resident(conv_w.shape),
                                                _resident(conv_b.shape)],
        out_specs=[pl.BlockSpec((tm, d_rnn), lambda i: (i, 0)),
                   pl.BlockSpec((tm, d_rnn), lambda i: (i, 0))],
        out_shape=[jax.ShapeDtypeStruct((m, d_rnn), BF16),
                   jax.ShapeDtypeStruct((m, d_rnn), BF16)],
        scratch_shapes=[pltpu.VMEM((tm + 2 * halo, d), BF16)],
        compiler_params=_params(1),
        name="in_proj",
    )(x2d, x2d, x2d, w_in, conv_w, conv_b)


def _gate_k_starts(d_rnn, n_blocks, ct, kw):
    bw = d_rnn // n_blocks
    starts = []
    for j in range(d_rnn // ct):
        lo = (j * ct) // bw * bw
        hi = ((j * ct + ct - 1) // bw + 1) * bw
        k0 = min(lo // 128 * 128, d_rnn - kw)
        assert k0 <= lo and hi <= k0 + kw
        starts.append(k0)
    return tuple(starts)


def _scan_tile(a, b, h0, reverse):
    tt, cw = a.shape
    groups = tt // F32_ROWS
    a3 = a.reshape(groups, F32_ROWS, cw)
    b3 = b.reshape(groups, F32_ROWS, cw)
    row = lax.broadcasted_iota(jnp.int32, a3.shape, 1)
    for k in (1, 2, 4):
        if reverse:
            shift, keep = F32_ROWS - k, row < F32_ROWS - k
        else:
            shift, keep = k, row >= k
        ar = pltpu.roll(a3, shift, 1)
        br = pltpu.roll(b3, shift, 1)
        b3 = b3 + a3 * jnp.where(keep, br, 0.0)
        a3 = a3 * jnp.where(keep, ar, 1.0)
    out = [None] * groups
    h = h0
    for g in (reversed(range(groups)) if reverse else range(groups)):
        hg = b3[g] + a3[g] * h
        out[g] = hg
        h = hg[0:1] if reverse else hg[F32_ROWS - 1:F32_ROWS]
    return jnp.concatenate(out, axis=0), h


def _lru_tile(u_ref, wg_ref, ba_ref, bx_ref, lam_ref, carry_ref, emit, *, reverse, k_starts,
              kw, ct):
    @pl.when(pl.program_id(1) == 0)
    def _():
        carry_ref[...] = jnp.zeros_like(carry_ref)

    lam = lam_ref[...]
    neg_lam = -lam
    softplus = jnp.maximum(neg_lam, 0.0) + jnp.log1p(jnp.exp(-jnp.abs(neg_lam)))
    decay = -LRU_C * softplus

    for j, k0 in enumerate(k_starts):
        cols = slice(j * ct, (j + 1) * ct)
        g = jnp.dot(u_ref[:, k0:k0 + kw], wg_ref[j], preferred_element_type=F32)
        r = _sigmoid(g[:, :ct] + ba_ref[:, cols])
        i = _sigmoid(g[:, ct:] + bx_ref[:, cols])
        log_a = decay[:, cols] * r
        a = jnp.exp(log_a)
        gain_sq = -jnp.tanh(log_a) * (a * a + 1.0)
        gain = jnp.where(gain_sq > 0.0, gain_sq * lax.rsqrt(gain_sq), 0.0)
        b = gain * (i * u_ref[:, cols].astype(F32))
        h, h_last = _scan_tile(a, b, carry_ref[0:1, cols], reverse)
        carry_ref[0:1, cols] = h_last
        emit(j, cols, h)


def _lru_fwd_kernel(u_ref, wg_ref, ba_ref, bx_ref, lam_ref, h_ref, carry_ref, **kw):
    def emit(j, cols, h):
        h_ref[:, cols] = h.astype(BF16)

    _lru_tile(u_ref, wg_ref, ba_ref, bx_ref, lam_ref, carry_ref, emit, reverse=False, **kw)


def _lru_bwd_kernel(u_ref, wg_ref, ba_ref, bx_ref, lam_ref, hf_ref, gate_ref, x_ref, wout_ref,
                    lng_ref, lnb_ref, out_ref, carry_ref, y_ref, **kw):
    def emit(j, cols, h):
        y = gate_ref[:, cols].astype(F32) * (hf_ref[:, cols].astype(F32) + h)
        y_ref[:, cols] = y.astype(BF16)

    _lru_tile(u_ref, wg_ref, ba_ref, bx_ref, lam_ref, carry_ref, emit, reverse=True, **kw)
    mix = jnp.dot(y_ref[...], wout_ref[...], preferred_element_type=F32)
    z = DN_ALPHA * x_ref[...] + mix
    out_ref[...] = _layer_norm(z, lng_ref[...], lnb_ref[...])


def _lru_fwd(u, wg, b_a, b_x, lam, k_starts, tt=256, ct=256, kw=512):
    batch, seq, d_rnn = u.shape
    cur = pl.BlockSpec((None, tt, d_rnn), lambda b, i: (b, i, 0))
    kern = functools.partial(_lru_fwd_kernel, k_starts=k_starts, kw=kw, ct=ct)
    return pl.pallas_call(
        kern,
        grid=(batch, seq // tt),
        in_specs=[cur, _resident(wg.shape), _resident(b_a.shape), _resident(b_x.shape),
                  _resident(lam.shape)],
        out_specs=cur,
        out_shape=jax.ShapeDtypeStruct((batch, seq, d_rnn), BF16),
        scratch_shapes=[pltpu.VMEM((F32_ROWS, d_rnn), F32)],
        compiler_params=_params(2),
        name="lru_fwd",
    )(u, wg, b_a, b_x, lam)


def _lru_bwd(u, wg, b_a, b_x, lam, h_fwd, gate, x, w_out, ln_g, ln_b, k_starts,
             tt=256, ct=256, kw=512):
    batch, seq, d_rnn = u.shape
    d = x.shape[-1]
    n_tiles = seq // tt
    cur = pl.BlockSpec((None, tt, d_rnn), lambda b, i: (b, n_tiles - 1 - i, 0))
    xspec = pl.BlockSpec((None, tt, d), lambda b, i: (b, n_tiles - 1 - i, 0))
    kern = functools.partial(_lru_bwd_kernel, k_starts=k_starts, kw=kw, ct=ct)
    return pl.pallas_call(
        kern,
        grid=(batch, n_tiles),
        in_specs=[cur, _resident(wg.shape), _resident(b_a.shape), _resident(b_x.shape),
                  _resident(lam.shape), cur, cur, xspec, _resident(w_out.shape),
                  _resident(ln_g.shape), _resident(ln_b.shape)],
        out_specs=xspec,
        out_shape=jax.ShapeDtypeStruct((batch, seq, d), F32),
        scratch_shapes=[pltpu.VMEM((F32_ROWS, d_rnn), F32), pltpu.VMEM((tt, d_rnn), BF16)],
        compiler_params=_params(2),
        name="lru_bwd",
    )(u, wg, b_a, b_x, lam, h_fwd, gate, x, w_out, ln_g, ln_b)


def _gate_weights(w_a, w_x, k_starts, kw, ct):
    nb, bw, _ = w_a.shape
    d_rnn = nb * bw
    eye = jnp.eye(nb, dtype=w_a.dtype)

    def dense(w):
        return jnp.einsum('ncd,nm->ncmd', w, eye).reshape(d_rnn, d_rnn)

    wa, wx = dense(w_a), dense(w_x)
    tiles = [jnp.concatenate([wa[k0:k0 + kw, j * ct:(j + 1) * ct],
                              wx[k0:k0 + kw, j * ct:(j + 1) * ct]], axis=1)
             for j, k0 in enumerate(k_starts)]
    return jnp.stack(tiles).astype(BF16)


def _ffn_kernel(x_prev_ref, x_ref, x_next_ref, wup_ref, cw_ref, cb_ref, wdn_ref, lng_ref,
                lnb_ref, out_ref, xe_ref, acc_ref, hm_ref, *, tiles_per_seq, d_ff, cf,
                down_every):
    tm = x_ref.shape[0]
    halo = x_prev_ref.shape[0]
    _fill_with_halo(xe_ref, x_prev_ref, x_ref, x_next_ref, pl.program_id(0), tiles_per_seq)
    n = tm + 2 * halo
    n_chunks = d_ff // cf

    def down(lo, hi):
        part = jnp.dot(hm_ref[:, lo * cf:hi * cf], wdn_ref[lo * cf:hi * cf, :],
                       preferred_element_type=F32)
        if lo == 0:
            acc_ref[...] = part
        else:
            acc_ref[...] += part

    pending = None
    start = 0
    for c in range(n_chunks):
        cols = slice(c * cf, (c + 1) * cf)
        g = jnp.dot(xe_ref[...], wup_ref[:, d_ff + c * cf:d_ff + (c + 1) * cf],
                    preferred_element_type=F32)
        v = jnp.dot(xe_ref[halo:halo + tm, :], wup_ref[:, cols], preferred_element_type=F32)
        if pending is not None:
            down(*pending)
            pending = None
        gc = (pltpu.roll(g, 1, 0) * cw_ref[0:1, cols] + g * cw_ref[1:2, cols]
              + pltpu.roll(g, n - 1, 0) * cw_ref[2:3, cols] + cb_ref[:, cols])[halo:halo + tm]
        hm_ref[:, cols] = (_gelu(gc) * v).astype(BF16)
        if (c + 1) % down_every == 0 or c + 1 == n_chunks:
            pending = (start, c + 1)
            start = c + 1
    down(*pending)
    z = DN_ALPHA * x_ref[...] + acc_ref[...]
    out_ref[...] = _layer_norm(z, lng_ref[...], lnb_ref[...])


def _conv_ffn(x2d, seq, w_up, conv_w, conv_b, w_down, ln_g, ln_b, tm=512, cf=256,
              down_every=6):
    m, d = x2d.shape
    d_ff = w_down.shape[0]
    halo = BF16_ROWS
    kern = functools.partial(_ffn_kernel, tiles_per_seq=seq // tm, d_ff=d_ff, cf=cf,
                             down_every=down_every)
    return pl.pallas_call(
        kern,
        grid=(m // tm,),
        in_specs=_halo_specs(m, d, tm, halo) + [
            _resident(w_up.shape), _resident(conv_w.shape), _resident(conv_b.shape),
            _resident(w_down.shape), _resident(ln_g.shape), _resident(ln_b.shape)],
        out_specs=pl.BlockSpec((tm, d), lambda i: (i, 0)),
        out_shape=jax.ShapeDtypeStruct((m, d), F32),
        scratch_shapes=[pltpu.VMEM((tm + 2 * halo, d), BF16), pltpu.VMEM((tm, d), F32),
                        pltpu.VMEM((tm, d_ff), BF16)],
        compiler_params=_params(1),
        name="conv_ffn",
    )(x2d, x2d, x2d, w_up, conv_w, conv_b, w_down, ln_g, ln_b)


def _class_permutation(tm, dil):
    out_row = jnp.arange(tm)
    src = (out_row % (tm // dil)) * dil + out_row // (tm // dil)
    return (src[:, None] == jnp.arange(tm)[None, :]).astype(BF16)


def _qkv_kernel(x_ref, w_ref, perm_ref, o0_ref, o1_ref, o2_ref, xb_ref, *, cn, dils, strided,
                width):
    out_refs = (o0_ref, o1_ref, o2_ref)
    tm = x_ref.shape[0]
    xb_ref[0] = x_ref[...].astype(BF16)
    lhs_of = {1: 0}
    for slot, dil in enumerate(strided, start=1):
        xb_ref[slot] = jnp.dot(perm_ref[slot - 1], xb_ref[0],
                               preferred_element_type=F32).astype(BF16)
        lhs_of[dil] = slot
    for j in range(w_ref.shape[1] // cn):
        c0 = j * cn
        g, t, off = c0 // (3 * width), (c0 % (3 * width)) // width, c0 % width
        dil = dils[g]
        res = jnp.dot(xb_ref[lhs_of[dil]], w_ref[:, c0:c0 + cn], preferred_element_type=F32)
        rows = tm // dil
        for r in range(dil):
            out_refs[g][t, r, :, off:off + cn] = res[r * rows:(r + 1) * rows].astype(BF16)


def _qkv_proj(x2d, w_qkv, batch, seq, dils, tm=512, cn=512):
    m, d = x2d.shape
    width = w_qkv.shape[1] // (3 * len(dils))
    tiles_per_seq = seq // tm
    strided = tuple(sorted({dil for dil in dils if dil != 1}))
    perms = jnp.stack([_class_permutation(tm, dil) for dil in strided])
    out_specs = [pl.BlockSpec((3, dil, tm // dil, width),
                              lambda i: (0, i // tiles_per_seq, i % tiles_per_seq, 0))
                 for dil in dils]
    out_shape = [jax.ShapeDtypeStruct((3, batch * dil, seq // dil, width), BF16) for dil in dils]
    return pl.pallas_call(
        functools.partial(_qkv_kernel, cn=cn, dils=dils, strided=strided, width=width),
        grid=(m // tm,),
        in_specs=[pl.BlockSpec((tm, d), lambda i: (i, 0)), _resident(w_qkv.shape),
                  _resident(perms.shape)],
        out_specs=out_specs,
        out_shape=out_shape,
        scratch_shapes=[pltpu.VMEM((1 + len(strided), tm, d), BF16)],
        compiler_params=_params(1),
        name="qkv_proj",
    )(x2d, w_qkv, perms)


def _attn_kernel(q_ref, kp_ref, kc_ref, kn_ref, vp_ref, vc_ref, vn_ref, o_ref, lse_ref,
                 bias_ref, *, n_q, half, dil):
    qi = pl.program_id(1)
    tq = q_ref.shape[0]
    win = tq + 2 * half
    lanes = 2 * HEAD_DIM

    @pl.when(jnp.logical_and(pl.program_id(0) == 0, qi == 0))
    def _():
        row = lax.broadcasted_iota(jnp.int32, (tq, win), 0)
        col = lax.broadcasted_iota(jnp.int32, (tq, win), 1)
        rel = col - half - row
        dist = jnp.abs(rel)
        in_band = dist <= half
        distf = (dist * dil).astype(F32)
        for variant in range(3):
            ok = in_band
            if variant == 0:
                ok = jnp.logical_and(ok, col >= half)
            if variant == 2:
                ok = jnp.logical_and(ok, col < half + tq)
            for h in range(N_HEADS):
                slope = 2.0 ** (-8.0 * (h + 1) / N_HEADS)
                rows = slice((h % 2) * tq, (h % 2 + 1) * tq)
                bias_ref[variant, h // 2, rows, :] = jnp.where(ok, -slope * distf, NEG)

    variant = jnp.where(qi == 0, 0, jnp.where(qi == n_q - 1, 2, 1))
    kwin = jnp.concatenate([kp_ref[...], kc_ref[...], kn_ref[...]], axis=0)
    vwin = jnp.concatenate([vp_ref[...], vc_ref[...], vn_ref[...]], axis=0)
    lane = lax.broadcasted_iota(jnp.int32, (1, lanes), 1)
    first_head = lane < HEAD_DIM
    lse_lane = lax.broadcasted_iota(jnp.int32, (tq, lanes), 1)
    lse_blk = jnp.zeros((tq, lanes), F32)
    zero = jnp.zeros((), BF16)
    scale = HEAD_DIM ** -0.5
    def scores(hp):
        cols = slice(hp * lanes, (hp + 1) * lanes)
        q2 = q_ref[:, cols] * jnp.asarray(scale, BF16)
        q_st = jnp.concatenate([jnp.where(first_head, q2, zero),
                                jnp.where(first_head, zero, q2)], axis=0)
        return lax.dot_general(q_st, kwin[:, cols], (((1,), (1,)), ((), ())),
                               preferred_element_type=F32)

    n_pairs = N_HEADS // 2
    s_next = scores(0)
    for hp in range(n_pairs):
        cols = slice(hp * lanes, (hp + 1) * lanes)
        s = s_next
        if hp + 1 < n_pairs:
            s_next = scores(hp + 1)
        s = s + bias_ref[variant, hp]
        m = jnp.max(s, axis=-1, keepdims=True)
        p = jnp.exp(s - m)
        den = jnp.sum(p, axis=-1, keepdims=True)
        o_st = jnp.dot(p.astype(BF16), vwin[:, cols], preferred_element_type=F32)
        o_st = o_st * (1.0 / den)
        o_ref[:, cols] = jnp.where(first_head, o_st[:tq], o_st[tq:]).astype(BF16)
        lse = m + jnp.log(den)
        lse_blk = jnp.where(lse_lane == 2 * hp, lse[:tq], lse_blk)
        lse_blk = jnp.where(lse_lane == 2 * hp + 1, lse[tq:], lse_blk)
    lse_ref[...] = lse_blk


def _band_attention(qkv, dil, half, tq=128):
    _, n_seq, length, width = qkv.shape
    n_q = length // tq
    hb = tq // half
    n_half = length // half
    assert n_q >= 2

    def cur(t):
        return pl.BlockSpec((None, None, tq, width), lambda s, i: (t, s, i, 0))

    def prev(t):
        return pl.BlockSpec((None, None, half, width),
                            lambda s, i: (t, s, jnp.maximum(i * hb - 1, 0), 0))

    def nxt(t):
        return pl.BlockSpec((None, None, half, width),
                            lambda s, i: (t, s, jnp.minimum((i + 1) * hb, n_half - 1), 0))

    o_spec = pl.BlockSpec((None, tq, width), lambda s, i: (s, i, 0))
    lse_spec = pl.BlockSpec((None, tq, LANES), lambda s, i: (s, i, 0))
    kern = functools.partial(_attn_kernel, n_q=n_q, half=half, dil=dil)
    return pl.pallas_call(
        kern,
        grid=(n_seq, n_q),
        in_specs=[cur(0), prev(1), cur(1), nxt(1), prev(2), cur(2), nxt(2)],
        out_specs=[o_spec, lse_spec],
        out_shape=[jax.ShapeDtypeStruct((n_seq, length, width), BF16),
                   jax.ShapeDtypeStruct((n_seq, length, LANES), F32)],
        scratch_shapes=[pltpu.VMEM((3, N_HEADS // 2, 2 * tq, tq + 2 * half), F32)],
        compiler_params=_params(2),
        name=f"band_attn_d{dil}",
    )(qkv, qkv, qkv, qkv, qkv, qkv, qkv)


def _merge_kernel(o0_ref, o1_ref, o2_ref, l0_ref, l1_ref, l2_ref, x_ref, e_ref, wo_ref,
                  lng_ref, lnb_ref, unperm_ref, out_ref, lscr_ref, acc_ref, *, dils, strided):
    o_refs = (o0_ref, o1_ref, o2_ref)
    l_refs = (l0_ref, l1_ref, l2_ref)
    tm, d = x_ref.shape
    lses = []
    for g, dil in enumerate(dils):
        if dil == 1:
            lses.append(l_refs[g][0])
            continue
        for r in range(dil):
            lscr_ref[g, pl.ds(r, tm // dil, stride=dil), :] = l_refs[g][r]
        lses.append(lscr_ref[g])
    top = jnp.maximum(jnp.maximum(lses[0], lses[1]), lses[2])
    ws = [jnp.exp(l - top) for l in lses]
    inv = 1.0 / (ws[0] + ws[1] + ws[2])
    for g, dil in enumerate(dils):
        w = ws[g] * inv
        w_hi = w.astype(BF16)
        w_lo = (w - w_hi.astype(F32)).astype(BF16)
        w_full = (jnp.dot(w_hi, e_ref[...], preferred_element_type=F32)
                  + jnp.dot(w_lo, e_ref[...], preferred_element_type=F32))
        if dil == 1:
            o_tok = o_refs[g][0].astype(F32)
        else:
            o_tok = jnp.dot(unperm_ref[strided.index(dil)], o_refs[g][...].reshape(tm, d),
                            preferred_element_type=F32)
        if g == 0:
            acc_ref[...] = w_full * o_tok
        else:
            acc_ref[...] += w_full * o_tok
    mix = jnp.dot(acc_ref[...].astype(BF16), wo_ref[...], preferred_element_type=F32)
    z = DN_ALPHA * x_ref[...] + mix
    out_ref[...] = _layer_norm(z, lng_ref[...], lnb_ref[...])


def _merge_out_proj(outs, lses, x2d, seq, dils, w_o, ln_g, ln_b, tm=512):
    m, d = x2d.shape
    tiles_per_seq = seq // tm
    head_of_col = jnp.arange(d) // HEAD_DIM
    expand = (jnp.arange(LANES)[:, None] == head_of_col[None, :]).astype(BF16)
    row = pl.BlockSpec((tm, d), lambda i: (i, 0))

    def classes(dil, width):
        return pl.BlockSpec((dil, tm // dil, width),
                            lambda i: (i // tiles_per_seq, i % tiles_per_seq, 0))

    strided = tuple(sorted({dil for dil in dils if dil != 1}))
    unperms = jnp.stack([_class_permutation(tm, dil).T for dil in strided])
    return pl.pallas_call(
        functools.partial(_merge_kernel, dils=dils, strided=strided),
        grid=(m // tm,),
        in_specs=([classes(dil, d) for dil in dils] + [classes(dil, LANES) for dil in dils]
                  + [row, _resident(expand.shape), _resident(w_o.shape),
                     _resident(ln_g.shape), _resident(ln_b.shape), _resident(unperms.shape)]),
        out_specs=row,
        out_shape=jax.ShapeDtypeStruct((m, d), F32),
        scratch_shapes=[pltpu.VMEM((len(dils), tm, LANES), F32),
                        pltpu.VMEM((tm, d), F32)],
        compiler_params=_params(1),
        name="merge_out_proj",
    )(*outs, *lses, x2d, expand, w_o, ln_g, ln_b, unperms)


def kernel(x, ln_g, ln_b, rg_w_in, rg_conv_w, rg_conv_b, rg_w_a, rg_b_a, rg_w_x, rg_b_x,
           rg_lam, rg_w_out, at_w_qkv, at_w_o, ff_w_up, ff_conv_w, ff_conv_b, ff_w_down):
    batch, seq, d = x.shape
    m = batch * seq
    d_rnn = rg_w_out.shape[1]
    d_attn = at_w_o.shape[1]

    def vec(p):
        return p.reshape(1, -1)

    ct, kw = MXU_TILE, 2 * MXU_TILE
    k_starts = _gate_k_starts(d_rnn, LRU_BLOCKS, ct, kw)
    gate, u = _in_proj(x.reshape(m, d), seq, rg_w_in[0].astype(BF16), rg_conv_w[0],
                       vec(rg_conv_b[0]))
    gate = gate.reshape(batch, seq, d_rnn)
    u = u.reshape(batch, seq, d_rnn)
    wg_f = _gate_weights(rg_w_a[0, 0], rg_w_x[0, 0], k_starts, kw, ct)
    wg_b = _gate_weights(rg_w_a[0, 1], rg_w_x[0, 1], k_starts, kw, ct)
    h_fwd = _lru_fwd(u, wg_f, vec(rg_b_a[0, 0]), vec(rg_b_x[0, 0]), vec(rg_lam[0, 0]),
                     k_starts, ct=ct, kw=kw)
    x1 = _lru_bwd(u, wg_b, vec(rg_b_a[0, 1]), vec(rg_b_x[0, 1]), vec(rg_lam[0, 1]), h_fwd, gate,
                  x, rg_w_out[0].astype(BF16), vec(ln_g[0, 0]), vec(ln_b[0, 0]), k_starts,
                  ct=ct, kw=kw)
    x2 = _conv_ffn(x1.reshape(m, d), seq, ff_w_up[0].astype(BF16), ff_conv_w[0],
                   vec(ff_conv_b[0]), ff_w_down[0].astype(BF16), vec(ln_g[0, 1]), vec(ln_b[0, 1]))

    dils = tuple(dil for _, dil in ATTN_GROUPS)
    qkv_groups = _qkv_proj(x2, at_w_qkv[0].astype(BF16), batch, seq, dils)
    outs, lses = [], []
    for qkv_g, (window, dil) in zip(qkv_groups, ATTN_GROUPS):
        o, lse = _band_attention(qkv_g, dil, window // (2 * dil))
        outs.append(o)
        lses.append(lse)
    x3 = _merge_out_proj(outs, lses, x2, seq, dils, at_w_o[0].astype(BF16),
                         vec(ln_g[1, 0]), vec(ln_b[1, 0]))
    x4 = _conv_ffn(x3, seq, ff_w_up[1].astype(BF16), ff_conv_w[1], vec(ff_conv_b[1]),
                   ff_w_down[1].astype(BF16), vec(ln_g[1, 1]), vec(ln_b[1, 1]))
    return x4.reshape(batch, seq, d)
```

```python
import functools
import math

import jax
import jax.numpy as jnp
from jax import lax
from jax.experimental import pallas as pl
from jax.experimental.pallas import tpu as pltpu

F32 = jnp.float32
BF16 = jnp.bfloat16

LRU_C = 8.0
LRU_BLOCKS = 16
ATTN_GROUPS = ((128, 1), (512, 4), (2048, 16))
N_HEADS = 16
HEAD_DIM = 64
LN_EPS = 1e-5
DEPTH = 2
DN_ALPHA = (2 * DEPTH) ** 0.25
NEG = -1e30

VMEM_LIMIT_BYTES = 56 * 1024 * 1024
MXU_TILE = 256
LANES = 128
BF16_ROWS = 16
F32_ROWS = 8

_GELU_C = math.sqrt(2.0 / math.pi)


def _gelu(x):
    inner = x * (_GELU_C + (_GELU_C * 0.044715) * (x * x))
    return x * (0.5 + 0.5 * jnp.tanh(inner))


def _layer_norm(z, g, b):
    mu = jnp.mean(z, axis=-1, keepdims=True)
    zc = z - mu
    var = jnp.mean(zc * zc, axis=-1, keepdims=True)
    return zc * lax.rsqrt(var + LN_EPS) * g + b


def _resident(shape):
    nd = len(shape)
    return pl.BlockSpec(shape, lambda *_: (0,) * nd, pipeline_mode=pl.Buffered(1))


def _params(n_axes):
    return pltpu.CompilerParams(
        dimension_semantics=("arbitrary",) * n_axes,
        vmem_limit_bytes=VMEM_LIMIT_BYTES)


def _fill_with_halo(xe_ref, x_prev_ref, x_ref, x_next_ref, i, tiles_per_seq):
    tm = x_ref.shape[0]
    halo = x_prev_ref.shape[0]
    first = (i % tiles_per_seq == 0)
    last = (i % tiles_per_seq == tiles_per_seq - 1)
    xe_ref[0:halo, :] = (x_prev_ref[...] * jnp.where(first, 0.0, 1.0)).astype(BF16)
    xe_ref[halo:halo + tm, :] = x_ref[...].astype(BF16)
    xe_ref[halo + tm:, :] = (x_next_ref[...] * jnp.where(last, 0.0, 1.0)).astype(BF16)


def _in_proj_kernel(x_prev_ref, x_ref, x_next_ref, w_ref, cw_ref, cb_ref, gate_ref, u_ref,
                    xe_ref, *, tiles_per_seq, d_rnn, cn):
    tm = x_ref.shape[0]
    halo = x_prev_ref.shape[0]
    n = tm + 2 * halo
    _fill_with_halo(xe_ref, x_prev_ref, x_ref, x_next_ref, pl.program_id(0), tiles_per_seq)
    def project(j):
        g = jnp.dot(xe_ref[halo:halo + tm, :], w_ref[:, j * cn:(j + 1) * cn],
                    preferred_element_type=F32)
        u = jnp.dot(xe_ref[...], w_ref[:, d_rnn + j * cn:d_rnn + (j + 1) * cn],
                    preferred_element_type=F32)
        return g, u

    n_chunks = d_rnn // cn
    ahead = project(0)
    for j in range(n_chunks):
        cols = slice(j * cn, (j + 1) * cn)
        g, u = ahead
        if j + 1 < n_chunks:
            ahead = project(j + 1)
        gate_ref[:, cols] = _gelu(g).astype(BF16)
        conv = (pltpu.roll(u, 2, 0) * cw_ref[0:1, cols] + pltpu.roll(u, 1, 0) * cw_ref[1:2, cols]
                + u * cw_ref[2:3, cols] + pltpu.roll(u, n - 1, 0) * cw_ref[3:4, cols]
                + cb_ref[:, cols])
        u_ref[:, cols] = conv[halo:halo + tm].astype(BF16)


def _halo_specs(m, d, tm, halo):
    hb = tm // halo
    n_halo = m // halo
    return [pl.BlockSpec((halo, d), lambda i: (jnp.maximum(i * hb - 1, 0), 0)),
            pl.BlockSpec((tm, d), lambda i: (i, 0)),
            pl.BlockSpec((halo, d), lambda i: (jnp.minimum((i + 1) * hb, n_halo - 1), 0))]


def _in_proj(x2d, seq, w_in, conv_w, conv_b, tm=512, cn=256):
    m, d = x2d.shape
    d_rnn = w_in.shape[1] // 2
    halo = BF16_ROWS
    kern = functools.partial(_in_proj_kernel, tiles_per_seq=seq // tm, d_rnn=d_rnn, cn=cn)
    return pl.pallas_call(
        kern,
        grid=(m // tm,),
        in_specs=_halo_specs(m, d, tm, halo) + [_resident(w_in.shape), _resident(conv_w.shape),
                                                _resident(conv_b.shape)],
        out_specs=[pl.BlockSpec((tm, d_rnn), lambda i: (i, 0)),
                   pl.BlockSpec((tm, d_rnn), lambda i: (i, 0))],
        out_shape=[jax.ShapeDtypeStruct((m, d_rnn), BF16),
                   jax.ShapeDtypeStruct((m, d_rnn), BF16)],
        scratch_shapes=[pltpu.VMEM((tm + 2 * halo, d), BF16)],
        compiler_params=_params(1),
        name="in_proj",
    )(x2d, x2d, x2d, w_in, conv_w, conv_b)


def _gate_k_starts(d_rnn, n_blocks, ct, kw):
    bw = d_rnn // n_blocks
    starts = []
    for j in range(d_rnn // ct):
        lo = (j * ct) // bw * bw
        hi = ((j * ct + ct - 1) // bw + 1) * bw
        k0 = min(lo // 128 * 128, d_rnn - kw)
        assert k0 <= lo and hi <= k0 + kw
        starts.append(k0)
    return tuple(starts)


def _scan_tile(a, b, h0, reverse):
    tt, cw = a.shape
    groups = tt // F32_ROWS
    a3 = a.reshape(groups, F32_ROWS, cw)
    b3 = b.reshape(groups, F32_ROWS, cw)
    row = lax.broadcasted_iota(jnp.int32, a3.shape, 1)
    for k in (1, 2, 4):
        if reverse:
            shift, keep = F32_ROWS - k, row < F32_ROWS - k
        else:
            shift, keep = k, row >= k
        ar = pltpu.roll(a3, shift, 1)
        br = pltpu.roll(b3, shift, 1)
        b3 = b3 + a3 * jnp.where(keep, br, 0.0)
        a3 = a3 * jnp.where(keep, ar, 1.0)
    out = [None] * groups
    h = h0
    for g in (reversed(range(groups)) if reverse else range(groups)):
        hg = b3[g] + a3[g] * h
        out[g] = hg
        h = hg[0:1] if reverse else hg[F32_ROWS - 1:F32_ROWS]
    return jnp.concatenate(out, axis=0), h


def _lru_tile(u_ref, wg_ref, ba_ref, bx_ref, lam_ref, carry_ref, emit, *, reverse, k_starts,
              kw, ct):
    @pl.when(pl.program_id(1) == 0)
    def _():
        carry_ref[...] = jnp.zeros_like(carry_ref)

    lam = lam_ref[...]
    neg_lam = -lam
    softplus = jnp.maximum(neg_lam, 0.0) + jnp.log1p(jnp.exp(-jnp.abs(neg_lam)))
    half_decay = (-0.5 * LRU_C) * softplus

    for j, k0 in enumerate(k_starts):
        cols = slice(j * ct, (j + 1) * ct)
        g = jnp.dot(u_ref[:, k0:k0 + kw], wg_ref[j], preferred_element_type=F32)
        tanh_r = jnp.tanh(g[:, :ct] + ba_ref[:, cols])
        i = 0.5 + 0.5 * jnp.tanh(g[:, ct:] + bx_ref[:, cols])
        log_a = half_decay[:, cols] + half_decay[:, cols] * tanh_r
        a = jnp.exp(log_a)
        gain_sq = -jnp.tanh(log_a) * (a * a + 1.0)
        gain = jnp.where(gain_sq > 0.0, gain_sq * lax.rsqrt(gain_sq), 0.0)
        b = gain * (i * u_ref[:, cols].astype(F32))
        h, h_last = _scan_tile(a, b, carry_ref[0:1, cols], reverse)
        carry_ref[0:1, cols] = h_last
        emit(j, cols, h)


def _lru_fwd_kernel(u_ref, wg_ref, ba_ref, bx_ref, lam_ref, h_ref, carry_ref, **kw):
    def emit(j, cols, h):
        h_ref[:, cols] = h.astype(BF16)

    _lru_tile(u_ref, wg_ref, ba_ref, bx_ref, lam_ref, carry_ref, emit, reverse=False, **kw)


def _lru_bwd_kernel(u_ref, wg_ref, ba_ref, bx_ref, lam_ref, hf_ref, gate_ref, x_ref, wout_ref,
                    lng_ref, lnb_ref, out_ref, carry_ref, y_ref, **kw):
    def emit(j, cols, h):
        y = gate_ref[:, cols].astype(F32) * (hf_ref[:, cols].astype(F32) + h)
        y_ref[:, cols] = y.astype(BF16)

    _lru_tile(u_ref, wg_ref, ba_ref, bx_ref, lam_ref, carry_ref, emit, reverse=True, **kw)
    mix = jnp.dot(y_ref[...], wout_ref[...], preferred_element_type=F32)
    z = DN_ALPHA * x_ref[...] + mix
    out_ref[...] = _layer_norm(z, lng_ref[...], lnb_ref[...])


def _lru_fwd(u, wg, b_a, b_x, lam, k_starts, tt=256, ct=256, kw=512):
    batch, seq, d_rnn = u.shape
    cur = pl.BlockSpec((None, tt, d_rnn), lambda b, i: (b, i, 0))
    kern = functools.partial(_lru_fwd_kernel, k_starts=k_starts, kw=kw, ct=ct)
    return pl.pallas_call(
        kern,
        grid=(batch, seq // tt),
        in_specs=[cur, _resident(wg.shape), _resident(b_a.shape), _resident(b_x.shape),
                  _resident(lam.shape)],
        out_specs=cur,
        out_shape=jax.ShapeDtypeStruct((batch, seq, d_rnn), BF16),
        scratch_shapes=[pltpu.VMEM((F32_ROWS, d_rnn), F32)],
        compiler_params=_params(2),
        name="lru_fwd",
    )(u, wg, b_a, b_x, lam)


def _lru_bwd(u, wg, b_a, b_x, lam, h_fwd, gate, x, w_out, ln_g, ln_b, k_starts,
             tt=256, ct=256, kw=512):
    batch, seq, d_rnn = u.shape
    d = x.shape[-1]
    n_tiles = seq // tt
    cur = pl.BlockSpec((None, tt, d_rnn), lambda b, i: (b, n_tiles - 1 - i, 0))
    xspec = pl.BlockSpec((None, tt, d), lambda b, i: (b, n_tiles - 1 - i, 0))
    kern = functools.partial(_lru_bwd_kernel, k_starts=k_starts, kw=kw, ct=ct)
    return pl.pallas_call(
        kern,
        grid=(batch, n_tiles),
        in_specs=[cur, _resident(wg.shape), _resident(b_a.shape), _resident(b_x.shape),
                  _resident(lam.shape), cur, cur, xspec, _resident(w_out.shape),
                  _resident(ln_g.shape), _resident(ln_b.shape)],
        out_specs=xspec,
        out_shape=jax.ShapeDtypeStruct((batch, seq, d), F32),
        scratch_shapes=[pltpu.VMEM((F32_ROWS, d_rnn), F32), pltpu.VMEM((tt, d_rnn), BF16)],
        compiler_params=_params(2),
        name="lru_bwd",
    )(u, wg, b_a, b_x, lam, h_fwd, gate, x, w_out, ln_g, ln_b)


def _gate_weights(w_a, w_x, k_starts, kw, ct):
    nb, bw, _ = w_a.shape
    d_rnn = nb * bw

    def dense(w):
        w = (0.5 * w).astype(BF16)
        return jnp.concatenate(
            [jnp.pad(w[n], ((0, 0), (n * bw, d_rnn - (n + 1) * bw))) for n in range(nb)], axis=0)

    wa, wx = dense(w_a), dense(w_x)
    tiles = [jnp.concatenate([wa[k0:k0 + kw, j * ct:(j + 1) * ct],
                              wx[k0:k0 + kw, j * ct:(j + 1) * ct]], axis=1)
             for j, k0 in enumerate(k_starts)]
    return jnp.stack(tiles).astype(BF16)


def _ffn_kernel(x_prev_ref, x_ref, x_next_ref, wup_ref, cw_ref, cb_ref, wdn_ref, lng_ref,
                lnb_ref, out_ref, xe_ref, acc_ref, hm_ref, *, tiles_per_seq, d_ff, cf,
                down_every):
    tm = x_ref.shape[0]
    halo = x_prev_ref.shape[0]
    _fill_with_halo(xe_ref, x_prev_ref, x_ref, x_next_ref, pl.program_id(0), tiles_per_seq)
    n = tm + 2 * halo
    n_chunks = d_ff // cf

    def down(lo, hi):
        part = jnp.dot(hm_ref[:, lo * cf:hi * cf], wdn_ref[lo * cf:hi * cf, :],
                       preferred_element_type=F32)
        if lo == 0:
            acc_ref[...] = part
        else:
            acc_ref[...] += part

    pending = None
    start = 0
    for c in range(n_chunks):
        cols = slice(c * cf, (c + 1) * cf)
        g = jnp.dot(xe_ref[...], wup_ref[:, d_ff + c * cf:d_ff + (c + 1) * cf],
                    preferred_element_type=F32)
        v = jnp.dot(xe_ref[halo:halo + tm, :], wup_ref[:, cols], preferred_element_type=F32)
        if pending is not None:
            down(*pending)
            pending = None
        gc = (pltpu.roll(g, 1, 0) * cw_ref[0:1, cols] + g * cw_ref[1:2, cols]
              + pltpu.roll(g, n - 1, 0) * cw_ref[2:3, cols] + cb_ref[:, cols])[halo:halo + tm]
        hm_ref[:, cols] = (_gelu(gc) * v).astype(BF16)
        if (c + 1) % down_every == 0 or c + 1 == n_chunks:
            pending = (start, c + 1)
            start = c + 1
    down(*pending)
    z = DN_ALPHA * x_ref[...] + acc_ref[...]
    out_ref[...] = _layer_norm(z, lng_ref[...], lnb_ref[...])


def _conv_ffn(x2d, seq, w_up, conv_w, conv_b, w_down, ln_g, ln_b, tm=512, cf=256,
              down_every=6):
    m, d = x2d.shape
    d_ff = w_down.shape[0]
    halo = BF16_ROWS
    kern = functools.partial(_ffn_kernel, tiles_per_seq=seq // tm, d_ff=d_ff, cf=cf,
                             down_every=down_every)
    return pl.pallas_call(
        kern,
        grid=(m // tm,),
        in_specs=_halo_specs(m, d, tm, halo) + [
            _resident(w_up.shape), _resident(conv_w.shape), _resident(conv_b.shape),
            _resident(w_down.shape), _resident(ln_g.shape), _resident(ln_b.shape)],
        out_specs=pl.BlockSpec((tm, d), lambda i: (i, 0)),
        out_shape=jax.ShapeDtypeStruct((m, d), F32),
        scratch_shapes=[pltpu.VMEM((tm + 2 * halo, d), BF16), pltpu.VMEM((tm, d), F32),
                        pltpu.VMEM((tm, d_ff), BF16)],
        compiler_params=_params(1),
        name="conv_ffn",
    )(x2d, x2d, x2d, w_up, conv_w, conv_b, w_down, ln_g, ln_b)


def _class_permutation(tm, dil):
    out_row = jnp.arange(tm)
    src = (out_row % (tm // dil)) * dil + out_row // (tm // dil)
    return (src[:, None] == jnp.arange(tm)[None, :]).astype(BF16)


def _qkv_kernel(x_ref, w_ref, perm_ref, o0_ref, o1_ref, o2_ref, xb_ref, *, cn, dils, strided,
                width):
    out_refs = (o0_ref, o1_ref, o2_ref)
    tm = x_ref.shape[0]
    xb_ref[0] = x_ref[...].astype(BF16)
    lhs_of = {1: 0}
    for slot, dil in enumerate(strided, start=1):
        xb_ref[slot] = jnp.dot(perm_ref[slot - 1], xb_ref[0],
                               preferred_element_type=F32).astype(BF16)
        lhs_of[dil] = slot
    for j in range(w_ref.shape[1] // cn):
        c0 = j * cn
        g, t, off = c0 // (3 * width), (c0 % (3 * width)) // width, c0 % width
        dil = dils[g]
        res = jnp.dot(xb_ref[lhs_of[dil]], w_ref[:, c0:c0 + cn], preferred_element_type=F32)
        rows = tm // dil
        for r in range(dil):
            out_refs[g][t, r, :, off:off + cn] = res[r * rows:(r + 1) * rows].astype(BF16)


def _qkv_proj(x2d, w_qkv, batch, seq, dils, tm=512, cn=512):
    m, d = x2d.shape
    width = w_qkv.shape[1] // (3 * len(dils))
    tiles_per_seq = seq // tm
    strided = tuple(sorted({dil for dil in dils if dil != 1}))
    perms = jnp.stack([_class_permutation(tm, dil) for dil in strided])
    out_specs = [pl.BlockSpec((3, dil, tm // dil, width),
                              lambda i: (0, i // tiles_per_seq, i % tiles_per_seq, 0))
                 for dil in dils]
    out_shape = [jax.ShapeDtypeStruct((3, batch * dil, seq // dil, width), BF16) for dil in dils]
    return pl.pallas_call(
        functools.partial(_qkv_kernel, cn=cn, dils=dils, strided=strided, width=width),
        grid=(m // tm,),
        in_specs=[pl.BlockSpec((tm, d), lambda i: (i, 0)), _resident(w_qkv.shape),
                  _resident(perms.shape)],
        out_specs=out_specs,
        out_shape=out_shape,
        scratch_shapes=[pltpu.VMEM((1 + len(strided), tm, d), BF16)],
        compiler_params=_params(1),
        name="qkv_proj",
    )(x2d, w_qkv, perms)


def _attn_kernel(q_ref, kp_ref, kc_ref, kn_ref, vp_ref, vc_ref, vn_ref, o_ref, lse_ref,
                 bias_ref, *, n_q, half, dil):
    qi = pl.program_id(1)
    tq = q_ref.shape[0]
    win = tq + 2 * half
    lanes = 2 * HEAD_DIM

    @pl.when(jnp.logical_and(pl.program_id(0) == 0, qi == 0))
    def _():
        row = lax.broadcasted_iota(jnp.int32, (tq, win), 0)
        col = lax.broadcasted_iota(jnp.int32, (tq, win), 1)
        rel = col - half - row
        dist = jnp.abs(rel)
        in_band = dist <= half
        distf = (dist * dil).astype(F32)
        for variant in range(3):
            ok = in_band
            if variant == 0:
                ok = jnp.logical_and(ok, col >= half)
            if variant == 2:
                ok = jnp.logical_and(ok, col < half + tq)
            for h in range(N_HEADS):
                slope = 2.0 ** (-8.0 * (h + 1) / N_HEADS)
                rows = slice((h % 2) * tq, (h % 2 + 1) * tq)
                bias_ref[variant, h // 2, rows, :] = jnp.where(ok, -slope * distf, NEG)

    variant = jnp.where(qi == 0, 0, jnp.where(qi == n_q - 1, 2, 1))
    kwin = jnp.concatenate([kp_ref[...], kc_ref[...], kn_ref[...]], axis=0)
    vwin = jnp.concatenate([vp_ref[...], vc_ref[...], vn_ref[...]], axis=0)
    lane = lax.broadcasted_iota(jnp.int32, (1, lanes), 1)
    first_head = lane < HEAD_DIM
    lse_lane = lax.broadcasted_iota(jnp.int32, (tq, lanes), 1)
    lse_blk = jnp.zeros((tq, lanes), F32)
    zero = jnp.zeros((), BF16)
    scale = HEAD_DIM ** -0.5
    for hp in range(N_HEADS // 2):
        cols = slice(hp * lanes, (hp + 1) * lanes)
        q2 = q_ref[:, cols] * jnp.asarray(scale, BF16)
        q_st = jnp.concatenate([jnp.where(first_head, q2, zero),
                                jnp.where(first_head, zero, q2)], axis=0)
        s = lax.dot_general(q_st, kwin[:, cols], (((1,), (1,)), ((), ())),
                            preferred_element_type=F32)
        s = s + bias_ref[variant, hp]
        m = jnp.max(s, axis=-1, keepdims=True)
        p = jnp.exp(s - m)
        den = jnp.sum(p, axis=-1, keepdims=True)
        o_st = jnp.dot(p.astype(BF16), vwin[:, cols], preferred_element_type=F32)
        o_st = o_st * (1.0 / den)
        o_ref[:, cols] = jnp.where(first_head, o_st[:tq], o_st[tq:]).astype(BF16)
        lse = m + jnp.log(den)
        lse_blk = jnp.where(lse_lane == 2 * hp, lse[:tq], lse_blk)
        lse_blk = jnp.where(lse_lane == 2 * hp + 1, lse[tq:], lse_blk)
    lse_ref[...] = lse_blk


def _band_attention(qkv, dil, half, tq=128):
    _, n_seq, length, width = qkv.shape
    n_q = length // tq
    hb = tq // half
    n_half = length // half
    assert n_q >= 2

    def cur(t):
        return pl.BlockSpec((None, None, tq, width), lambda s, i: (t, s, i, 0))

    def prev(t):
        return pl.BlockSpec((None, None, half, width),
                            lambda s, i: (t, s, jnp.maximum(i * hb - 1, 0), 0))

    def nxt(t):
        return pl.BlockSpec((None, None, half, width),
                            lambda s, i: (t, s, jnp.minimum((i + 1) * hb, n_half - 1), 0))

    o_spec = pl.BlockSpec((None, tq, width), lambda s, i: (s, i, 0))
    lse_spec = pl.BlockSpec((None, tq, LANES), lambda s, i: (s, i, 0))
    kern = functools.partial(_attn_kernel, n_q=n_q, half=half, dil=dil)
    return pl.pallas_call(
        kern,
        grid=(n_seq, n_q),
        in_specs=[cur(0), prev(1), cur(1), nxt(1), prev(2), cur(2), nxt(2)],
        out_specs=[o_spec, lse_spec],
        out_shape=[jax.ShapeDtypeStruct((n_seq, length, width), BF16),
                   jax.ShapeDtypeStruct((n_seq, length, LANES), F32)],
        scratch_shapes=[pltpu.VMEM((3, N_HEADS // 2, 2 * tq, tq + 2 * half), F32)],
        compiler_params=_params(2),
        name=f"band_attn_d{dil}",
    )(qkv, qkv, qkv, qkv, qkv, qkv, qkv)


def _merge_kernel(o0_ref, o1_ref, o2_ref, l0_ref, l1_ref, l2_ref, x_ref, e_ref, wo_ref,
                  lng_ref, lnb_ref, unperm_ref, out_ref, lscr_ref, acc_ref, *, dils, strided):
    o_refs = (o0_ref, o1_ref, o2_ref)
    l_refs = (l0_ref, l1_ref, l2_ref)
    tm, d = x_ref.shape
    o_toks = []
    for g, dil in enumerate(dils):
        if dil == 1:
            o_toks.append(None)
        else:
            o_toks.append(jnp.dot(unperm_ref[strided.index(dil)], o_refs[g][...].reshape(tm, d),
                                  preferred_element_type=F32))
    lses = []
    for g, dil in enumerate(dils):
        if dil == 1:
            lses.append(l_refs[g][0])
            continue
        for r in range(dil):
            lscr_ref[g, pl.ds(r, tm // dil, stride=dil), :] = l_refs[g][r]
        lses.append(lscr_ref[g])
    top = jnp.maximum(jnp.maximum(lses[0], lses[1]), lses[2])
    ws = [jnp.exp(l - top) for l in lses]
    inv = 1.0 / (ws[0] + ws[1] + ws[2])
    for g, dil in enumerate(dils):
        w = ws[g] * inv
        w_hi = w.astype(BF16)
        w_lo = (w - w_hi.astype(F32)).astype(BF16)
        w_full = jnp.dot(jnp.concatenate([w_hi, w_lo], axis=1), e_ref[...],
                         preferred_element_type=F32)
        o_tok = o_refs[g][0].astype(F32) if dil == 1 else o_toks[g]
        if g == 0:
            acc_ref[...] = w_full * o_tok
        else:
            acc_ref[...] += w_full * o_tok
    mix = jnp.dot(acc_ref[...].astype(BF16), wo_ref[...], preferred_element_type=F32)
    z = DN_ALPHA * x_ref[...] + mix
    out_ref[...] = _layer_norm(z, lng_ref[...], lnb_ref[...])


def _merge_out_proj(outs, lses, x2d, seq, dils, w_o, ln_g, ln_b, tm=512):
    m, d = x2d.shape
    tiles_per_seq = seq // tm
    head_of_col = jnp.arange(d) // HEAD_DIM
    expand = (jnp.arange(LANES)[:, None] == head_of_col[None, :]).astype(BF16)
    expand = jnp.concatenate([expand, expand], axis=0)
    row = pl.BlockSpec((tm, d), lambda i: (i, 0))

    def classes(dil, width):
        return pl.BlockSpec((dil, tm // dil, width),
                            lambda i: (i // tiles_per_seq, i % tiles_per_seq, 0))

    strided = tuple(sorted({dil for dil in dils if dil != 1}))
    unperms = jnp.stack([_class_permutation(tm, dil).T for dil in strided])
    return pl.pallas_call(
        functools.partial(_merge_kernel, dils=dils, strided=strided),
        grid=(m // tm,),
        in_specs=([classes(dil, d) for dil in dils] + [classes(dil, LANES) for dil in dils]
                  + [row, _resident(expand.shape), _resident(w_o.shape),
                     _resident(ln_g.shape), _resident(ln_b.shape), _resident(unperms.shape)]),
        out_specs=row,
        out_shape=jax.ShapeDtypeStruct((m, d), F32),
        scratch_shapes=[pltpu.VMEM((len(dils), tm, LANES), F32),
                        pltpu.VMEM((tm, d), F32)],
        compiler_params=_params(1),
        name="merge_out_proj",
    )(*outs, *lses, x2d, expand, w_o, ln_g, ln_b, unperms)


def kernel(x, ln_g, ln_b, rg_w_in, rg_conv_w, rg_conv_b, rg_w_a, rg_b_a, rg_w_x, rg_b_x,
           rg_lam, rg_w_out, at_w_qkv, at_w_o, ff_w_up, ff_conv_w, ff_conv_b, ff_w_down):
    batch, seq, d = x.shape
    m = batch * seq
    d_rnn = rg_w_out.shape[1]

    def vec(p):
        return p.reshape(1, -1)

    ct, kw = MXU_TILE, 2 * MXU_TILE
    k_starts = _gate_k_starts(d_rnn, LRU_BLOCKS, ct, kw)
    gate, u = _in_proj(x.reshape(m, d), seq, rg_w_in[0].astype(BF16), rg_conv_w[0],
                       vec(rg_conv_b[0]))
    gate = gate.reshape(batch, seq, d_rnn)
    u = u.reshape(batch, seq, d_rnn)
    wg_f = _gate_weights(rg_w_a[0, 0], rg_w_x[0, 0], k_starts, kw, ct)
    wg_b = _gate_weights(rg_w_a[0, 1], rg_w_x[0, 1], k_starts, kw, ct)
    h_fwd = _lru_fwd(u, wg_f, vec(0.5 * rg_b_a[0, 0]), vec(0.5 * rg_b_x[0, 0]), vec(rg_lam[0, 0]),
                     k_starts, ct=ct, kw=kw)
    x1 = _lru_bwd(u, wg_b, vec(0.5 * rg_b_a[0, 1]), vec(0.5 * rg_b_x[0, 1]), vec(rg_lam[0, 1]),
                  h_fwd, gate, x, rg_w_out[0].astype(BF16), vec(ln_g[0, 0]), vec(ln_b[0, 0]),
                  k_starts, ct=ct, kw=kw)
    x2 = _conv_ffn(x1.reshape(m, d), seq, ff_w_up[0].astype(BF16), ff_conv_w[0],
                   vec(ff_conv_b[0]), ff_w_down[0].astype(BF16), vec(ln_g[0, 1]), vec(ln_b[0, 1]))

    dils = tuple(dil for _, dil in ATTN_GROUPS)
    qkv_groups = _qkv_proj(x2, at_w_qkv[0].astype(BF16), batch, seq, dils)
    outs, lses = [], []
    for qkv_g, (window, dil) in zip(qkv_groups, ATTN_GROUPS):
        o, lse = _band_attention(qkv_g, dil, window // (2 * dil))
        outs.append(o)
        lses.append(lse)
    x3 = _merge_out_proj(outs, lses, x2, seq, dils, at_w_o[0].astype(BF16),
                         vec(ln_g[1, 0]), vec(ln_b[1, 0]))
    x4 = _conv_ffn(x3, seq, ff_w_up[1].astype(BF16), ff_conv_w[1], vec(ff_conv_b[1]),
                   ff_w_down[1].astype(BF16), vec(ln_g[1, 1]), vec(ln_b[1, 1]))
    return x4.reshape(batch, seq, d)
```

```python
import functools
import math

import jax
import jax.numpy as jnp
from jax import lax
from jax.experimental import pallas as pl
from jax.experimental.pallas import tpu as pltpu

F32 = jnp.float32
BF16 = jnp.bfloat16

LRU_C = 8.0
LRU_BLOCKS = 16
ATTN_GROUPS = ((128, 1), (512, 4), (2048, 16))
N_HEADS = 16
HEAD_DIM = 64
LN_EPS = 1e-5
DEPTH = 2
DN_ALPHA = (2 * DEPTH) ** 0.25
NEG = -1e30

VMEM_LIMIT_BYTES = 56 * 1024 * 1024
MXU_TILE = 256
LANES = 128
BF16_ROWS = 16
F32_ROWS = 8

_GELU_C = math.sqrt(2.0 / math.pi)


def _gelu(x):
    inner = x * (_GELU_C + (_GELU_C * 0.044715) * (x * x))
    return x * (0.5 + 0.5 * jnp.tanh(inner))


def _layer_norm(z, g, b):
    mu = jnp.mean(z, axis=-1, keepdims=True)
    zc = z - mu
    var = jnp.mean(zc * zc, axis=-1, keepdims=True)
    return zc * lax.rsqrt(var + LN_EPS) * g + b


def _resident(shape):
    nd = len(shape)
    return pl.BlockSpec(shape, lambda *_: (0,) * nd, pipeline_mode=pl.Buffered(1))


def _params(n_axes):
    return pltpu.CompilerParams(
        dimension_semantics=("arbitrary",) * n_axes,
        vmem_limit_bytes=VMEM_LIMIT_BYTES)


def _fill_with_halo(xe_ref, x_prev_ref, x_ref, x_next_ref, i, tiles_per_seq):
    tm = x_ref.shape[0]
    halo = x_prev_ref.shape[0]
    first = (i % tiles_per_seq == 0)
    last = (i % tiles_per_seq == tiles_per_seq - 1)
    xe_ref[0:halo, :] = (x_prev_ref[...] * jnp.where(first, 0.0, 1.0)).astype(BF16)
    xe_ref[halo:halo + tm, :] = x_ref[...].astype(BF16)
    xe_ref[halo + tm:, :] = (x_next_ref[...] * jnp.where(last, 0.0, 1.0)).astype(BF16)


def _in_proj_stages(x_prev_ref, x_ref, x_next_ref, w_ref, cw_ref, cb_ref, gate_ref, u_ref,
                    xe_ref, tile, tiles_per_seq, cn):
    tm = x_ref.shape[0]
    halo = x_prev_ref.shape[0]
    d_rnn = u_ref.shape[1]
    n = tm + 2 * halo
    _fill_with_halo(xe_ref, x_prev_ref, x_ref, x_next_ref, tile, tiles_per_seq)

    n_chunks = d_rnn // cn

    def stage(j):
        cols = slice(j * cn, (j + 1) * cn)
        g = jnp.dot(xe_ref[halo:halo + tm, :], w_ref[:, cols], preferred_element_type=F32)
        u = jnp.dot(xe_ref[...], w_ref[:, d_rnn + j * cn:d_rnn + (j + 1) * cn],
                    preferred_element_type=F32)
        gate_ref[:, cols] = _gelu(g).astype(BF16)
        conv = (pltpu.roll(u, 2, 0) * cw_ref[0:1, cols] + pltpu.roll(u, 1, 0) * cw_ref[1:2, cols]
                + u * cw_ref[2:3, cols] + pltpu.roll(u, n - 1, 0) * cw_ref[3:4, cols]
                + cb_ref[:, cols])
        u_ref[:, cols] = conv[halo:halo + tm].astype(BF16)

    return stage, n_chunks


def _halo_specs(m, d, tm, halo):
    hb = tm // halo
    n_halo = m // halo
    return [pl.BlockSpec((halo, d), lambda i: (jnp.maximum(i * hb - 1, 0), 0)),
            pl.BlockSpec((tm, d), lambda i: (i, 0)),
            pl.BlockSpec((halo, d), lambda i: (jnp.minimum((i + 1) * hb, n_halo - 1), 0))]


def _gate_k_starts(d_rnn, n_blocks, ct, kw):
    bw = d_rnn // n_blocks
    starts = []
    for j in range(d_rnn // ct):
        lo = (j * ct) // bw * bw
        hi = ((j * ct + ct - 1) // bw + 1) * bw
        k0 = min(lo // 128 * 128, d_rnn - kw)
        assert k0 <= lo and hi <= k0 + kw
        starts.append(k0)
    return tuple(starts)


def _scan_tile(a, b, h0, reverse):
    tt, cw = a.shape
    groups = tt // F32_ROWS
    a3 = a.reshape(groups, F32_ROWS, cw)
    b3 = b.reshape(groups, F32_ROWS, cw)
    row = lax.broadcasted_iota(jnp.int32, a3.shape, 1)
    for k in (1, 2, 4):
        if reverse:
            shift, keep = F32_ROWS - k, row < F32_ROWS - k
        else:
            shift, keep = k, row >= k
        ar = pltpu.roll(a3, shift, 1)
        br = pltpu.roll(b3, shift, 1)
        b3 = b3 + a3 * jnp.where(keep, br, 0.0)
        a3 = a3 * jnp.where(keep, ar, 1.0)
    out = [None] * groups
    h = h0
    for g in (reversed(range(groups)) if reverse else range(groups)):
        hg = b3[g] + a3[g] * h
        out[g] = hg
        h = hg[0:1] if reverse else hg[F32_ROWS - 1:F32_ROWS]
    return jnp.concatenate(out, axis=0), h


def _lru_tile(u_ref, wg_ref, ba_ref, bx_ref, lam_ref, carry_ref, emit, *, reverse, k_starts,
              kw, ct, before_tile=None):
    @pl.when(pl.program_id(1) == 0)
    def _():
        carry_ref[...] = jnp.zeros_like(carry_ref)

    lam = lam_ref[...]
    neg_lam = -lam
    softplus = jnp.maximum(neg_lam, 0.0) + jnp.log1p(jnp.exp(-jnp.abs(neg_lam)))
    half_decay = (-0.5 * LRU_C) * softplus

    for j, k0 in enumerate(k_starts):
        if before_tile is not None:
            before_tile(j)
        cols = slice(j * ct, (j + 1) * ct)
        g = jnp.dot(u_ref[:, k0:k0 + kw], wg_ref[j], preferred_element_type=F32)
        tanh_r = jnp.tanh(g[:, :ct] + ba_ref[:, cols])
        i = 0.5 + 0.5 * jnp.tanh(g[:, ct:] + bx_ref[:, cols])
        log_a = half_decay[:, cols] + half_decay[:, cols] * tanh_r
        a = jnp.exp(log_a)
        gain_sq = -jnp.tanh(log_a) * (a * a + 1.0)
        gain = jnp.where(gain_sq > 0.0, gain_sq * lax.rsqrt(gain_sq), 0.0)
        b = gain * (i * u_ref[:, cols].astype(F32))
        h, h_last = _scan_tile(a, b, carry_ref[0:1, cols], reverse)
        carry_ref[0:1, cols] = h_last
        emit(j, cols, h)


def _lru_fwd_kernel(x0_prev_ref, x0_ref, x0_next_ref, x_prev_ref, x_ref, x_next_ref, w_ref, cw_ref,
                    cb_ref, wg_ref, ba_ref, bx_ref, lam_ref, gate_ref, u_ref, h_ref, xe_ref,
                    gate_buf, u_buf, carry_ref, *, n_tiles, cn, **kw_args):
    i = pl.program_id(1)
    slot = i % 2

    @pl.when(i == 0)
    def _():
        stage, n_chunks = _in_proj_stages(x0_prev_ref, x0_ref, x0_next_ref, w_ref, cw_ref, cb_ref,
                                          gate_buf.at[0], u_buf.at[0], xe_ref, 0, n_tiles, cn)
        for j in range(n_chunks):
            stage(j)

    ahead = jnp.minimum(i + 1, n_tiles - 1)

    def emit(j, cols, h):
        h_ref[:, cols] = h.astype(BF16)

    def step(cur, nxt):
        stage, n_chunks = _in_proj_stages(x_prev_ref, x_ref, x_next_ref, w_ref, cw_ref, cb_ref,
                                          gate_buf.at[nxt], u_buf.at[nxt], xe_ref, ahead,
                                          n_tiles, cn)
        assert n_chunks <= len(kw_args["k_starts"])

        def before_tile(j):
            if j < n_chunks:
                stage(j)

        _lru_tile(u_buf.at[cur], wg_ref, ba_ref, bx_ref, lam_ref, carry_ref, emit,
                  reverse=False, before_tile=before_tile, **kw_args)
        gate_ref[...] = gate_buf[cur]
        u_ref[...] = u_buf[cur]

    @pl.when(slot == 0)
    def _():
        step(0, 1)

    @pl.when(slot == 1)
    def _():
        step(1, 0)


def _lru_bwd_kernel(u_ref, wg_ref, ba_ref, bx_ref, lam_ref, hf_ref, gate_ref, x_ref, wout_ref,
                    lng_ref, lnb_ref, out_ref, carry_ref, y_ref, **kw):
    def emit(j, cols, h):
        y = gate_ref[:, cols].astype(F32) * (hf_ref[:, cols].astype(F32) + h)
        y_ref[:, cols] = y.astype(BF16)

    _lru_tile(u_ref, wg_ref, ba_ref, bx_ref, lam_ref, carry_ref, emit, reverse=True, **kw)
    mix = jnp.dot(y_ref[...], wout_ref[...], preferred_element_type=F32)
    z = DN_ALPHA * x_ref[...] + mix
    out_ref[...] = _layer_norm(z, lng_ref[...], lnb_ref[...])


def _in_proj_lru_fwd(x, w_in, conv_w, conv_b, wg, b_a, b_x, lam, k_starts, tt=256, cn=256,
                     ct=256, kw=512):
    batch, seq, d = x.shape
    d_rnn = w_in.shape[1] // 2
    n_tiles = seq // tt
    halo = BF16_ROWS
    hb = tt // halo
    n_halo = seq // halo
    def ahead(i):
        return jnp.minimum(i + 1, n_tiles - 1)

    x_specs = [pl.BlockSpec((None, halo, d), lambda b, i: (b, 0, 0)),
               pl.BlockSpec((None, tt, d), lambda b, i: (b, 0, 0)),
               pl.BlockSpec((None, halo, d), lambda b, i: (b, jnp.minimum(hb, n_halo - 1), 0)),
               pl.BlockSpec((None, halo, d),
                            lambda b, i: (b, jnp.maximum(ahead(i) * hb - 1, 0), 0)),
               pl.BlockSpec((None, tt, d), lambda b, i: (b, ahead(i), 0)),
               pl.BlockSpec((None, halo, d),
                            lambda b, i: (b, jnp.minimum((ahead(i) + 1) * hb, n_halo - 1), 0))]
    cur = pl.BlockSpec((None, tt, d_rnn), lambda b, i: (b, i, 0))
    out = jax.ShapeDtypeStruct((batch, seq, d_rnn), BF16)
    kern = functools.partial(_lru_fwd_kernel, n_tiles=n_tiles, cn=cn, k_starts=k_starts, kw=kw,
                             ct=ct)
    return pl.pallas_call(
        kern,
        grid=(batch, n_tiles),
        in_specs=x_specs + [_resident(w_in.shape), _resident(conv_w.shape),
                            _resident(conv_b.shape), _resident(wg.shape), _resident(b_a.shape),
                            _resident(b_x.shape), _resident(lam.shape)],
        out_specs=[cur, cur, cur],
        out_shape=[out, out, out],
        scratch_shapes=[pltpu.VMEM((tt + 2 * halo, d), BF16),
                        pltpu.VMEM((2, tt, d_rnn), BF16), pltpu.VMEM((2, tt, d_rnn), BF16),
                        pltpu.VMEM((F32_ROWS, d_rnn), F32)],
        compiler_params=_params(2),
        name="in_proj_lru_fwd",
    )(x, x, x, x, x, x, w_in, conv_w, conv_b, wg, b_a, b_x, lam)


def _lru_bwd(u, wg, b_a, b_x, lam, h_fwd, gate, x, w_out, ln_g, ln_b, k_starts,
             tt=256, ct=256, kw=512):
    batch, seq, d_rnn = u.shape
    d = x.shape[-1]
    n_tiles = seq // tt
    cur = pl.BlockSpec((None, tt, d_rnn), lambda b, i: (b, n_tiles - 1 - i, 0))
    xspec = pl.BlockSpec((None, tt, d), lambda b, i: (b, n_tiles - 1 - i, 0))
    kern = functools.partial(_lru_bwd_kernel, k_starts=k_starts, kw=kw, ct=ct)
    return pl.pallas_call(
        kern,
        grid=(batch, n_tiles),
        in_specs=[cur, _resident(wg.shape), _resident(b_a.shape), _resident(b_x.shape),
                  _resident(lam.shape), cur, cur, xspec, _resident(w_out.shape),
                  _resident(ln_g.shape), _resident(ln_b.shape)],
        out_specs=xspec,
        out_shape=jax.ShapeDtypeStruct((batch, seq, d), F32),
        scratch_shapes=[pltpu.VMEM((F32_ROWS, d_rnn), F32), pltpu.VMEM((tt, d_rnn), BF16)],
        compiler_params=_params(2),
        name="lru_bwd",
    )(u, wg, b_a, b_x, lam, h_fwd, gate, x, w_out, ln_g, ln_b)


def _gate_weights(w_a, w_x, k_starts, kw, ct):
    nb, bw, _ = w_a.shape
    d_rnn = nb * bw

    def dense(w):
        w = (0.5 * w).astype(BF16)
        return jnp.concatenate(
            [jnp.pad(w[n], ((0, 0), (n * bw, d_rnn - (n + 1) * bw))) for n in range(nb)], axis=0)

    wa, wx = dense(w_a), dense(w_x)
    tiles = [jnp.concatenate([wa[k0:k0 + kw, j * ct:(j + 1) * ct],
                              wx[k0:k0 + kw, j * ct:(j + 1) * ct]], axis=1)
             for j, k0 in enumerate(k_starts)]
    return jnp.stack(tiles).astype(BF16)


def _ffn_kernel(x_prev_ref, x_ref, x_next_ref, wup_ref, cw_ref, cb_ref, wdn_ref, lng_ref,
                lnb_ref, out_ref, xe_ref, acc_ref, hm_ref, *, tiles_per_seq, d_ff, cf,
                down_every):
    tm = x_ref.shape[0]
    halo = x_prev_ref.shape[0]
    _fill_with_halo(xe_ref, x_prev_ref, x_ref, x_next_ref, pl.program_id(0), tiles_per_seq)
    n = tm + 2 * halo
    n_chunks = d_ff // cf

    def down(lo, hi):
        part = jnp.dot(hm_ref[:, lo * cf:hi * cf], wdn_ref[lo * cf:hi * cf, :],
                       preferred_element_type=F32)
        if lo == 0:
            acc_ref[...] = part
        else:
            acc_ref[...] += part

    pending = None
    start = 0
    for c in range(n_chunks):
        cols = slice(c * cf, (c + 1) * cf)
        g = jnp.dot(xe_ref[...], wup_ref[:, d_ff + c * cf:d_ff + (c + 1) * cf],
                    preferred_element_type=F32)
        v = jnp.dot(xe_ref[halo:halo + tm, :], wup_ref[:, cols], preferred_element_type=F32)
        if pending is not None:
            down(*pending)
            pending = None
        gc = (pltpu.roll(g, 1, 0) * cw_ref[0:1, cols] + g * cw_ref[1:2, cols]
              + pltpu.roll(g, n - 1, 0) * cw_ref[2:3, cols] + cb_ref[:, cols])[halo:halo + tm]
        hm_ref[:, cols] = (_gelu(gc) * v).astype(BF16)
        if (c + 1) % down_every == 0 or c + 1 == n_chunks:
            pending = (start, c + 1)
            start = c + 1
    down(*pending)
    z = DN_ALPHA * x_ref[...] + acc_ref[...]
    out_ref[...] = _layer_norm(z, lng_ref[...], lnb_ref[...])


def _conv_ffn(x2d, seq, w_up, conv_w, conv_b, w_down, ln_g, ln_b, tm=512, cf=256,
              down_every=6):
    m, d = x2d.shape
    d_ff = w_down.shape[0]
    halo = BF16_ROWS
    kern = functools.partial(_ffn_kernel, tiles_per_seq=seq // tm, d_ff=d_ff, cf=cf,
                             down_every=down_every)
    return pl.pallas_call(
        kern,
        grid=(m // tm,),
        in_specs=_halo_specs(m, d, tm, halo) + [
            _resident(w_up.shape), _resident(conv_w.shape), _resident(conv_b.shape),
            _resident(w_down.shape), _resident(ln_g.shape), _resident(ln_b.shape)],
        out_specs=pl.BlockSpec((tm, d), lambda i: (i, 0)),
        out_shape=jax.ShapeDtypeStruct((m, d), F32),
        scratch_shapes=[pltpu.VMEM((tm + 2 * halo, d), BF16), pltpu.VMEM((tm, d), F32),
                        pltpu.VMEM((tm, d_ff), BF16)],
        compiler_params=_params(1),
        name="conv_ffn",
    )(x2d, x2d, x2d, w_up, conv_w, conv_b, w_down, ln_g, ln_b)


def _class_permutation(tm, dil):
    out_row = jnp.arange(tm)
    src = (out_row % (tm // dil)) * dil + out_row // (tm // dil)
    return (src[:, None] == jnp.arange(tm)[None, :]).astype(BF16)


def _qkv_kernel(x_ref, w_ref, perm_ref, o0_ref, o1_ref, o2_ref, xb_ref, *, cn, dils, strided,
                width):
    out_refs = (o0_ref, o1_ref, o2_ref)
    tm = x_ref.shape[0]
    xb_ref[0] = x_ref[...].astype(BF16)
    lhs_of = {1: 0}
    for slot, dil in enumerate(strided, start=1):
        xb_ref[slot] = jnp.dot(perm_ref[slot - 1], xb_ref[0],
                               preferred_element_type=F32).astype(BF16)
        lhs_of[dil] = slot
    for j in range(w_ref.shape[1] // cn):
        c0 = j * cn
        g, t, off = c0 // (3 * width), (c0 % (3 * width)) // width, c0 % width
        dil = dils[g]
        res = jnp.dot(xb_ref[lhs_of[dil]], w_ref[:, c0:c0 + cn], preferred_element_type=F32)
        rows = tm // dil
        for r in range(dil):
            out_refs[g][t, r, :, off:off + cn] = res[r * rows:(r + 1) * rows].astype(BF16)


def _qkv_proj(x2d, w_qkv, batch, seq, dils, tm=512, cn=512):
    m, d = x2d.shape
    width = w_qkv.shape[1] // (3 * len(dils))
    tiles_per_seq = seq // tm
    strided = tuple(sorted({dil for dil in dils if dil != 1}))
    perms = jnp.stack([_class_permutation(tm, dil) for dil in strided])
    out_specs = [pl.BlockSpec((3, dil, tm // dil, width),
                              lambda i: (0, i // tiles_per_seq, i % tiles_per_seq, 0))
                 for dil in dils]
    out_shape = [jax.ShapeDtypeStruct((3, batch * dil, seq // dil, width), BF16) for dil in dils]
    return pl.pallas_call(
        functools.partial(_qkv_kernel, cn=cn, dils=dils, strided=strided, width=width),
        grid=(m // tm,),
        in_specs=[pl.BlockSpec((tm, d), lambda i: (i, 0)), _resident(w_qkv.shape),
                  _resident(perms.shape)],
        out_specs=out_specs,
        out_shape=out_shape,
        scratch_shapes=[pltpu.VMEM((1 + len(strided), tm, d), BF16)],
        compiler_params=_params(1),
        name="qkv_proj",
    )(x2d, w_qkv, perms)


def _attn_kernel(q_ref, kp_ref, kc_ref, kn_ref, vp_ref, vc_ref, vn_ref, o_ref, lse_ref,
                 bias_ref, *, n_q, half, dil):
    qi = pl.program_id(1)
    tq = q_ref.shape[0]
    win = tq + 2 * half
    lanes = 2 * HEAD_DIM

    @pl.when(jnp.logical_and(pl.program_id(0) == 0, qi == 0))
    def _():
        row = lax.broadcasted_iota(jnp.int32, (tq, win), 0)
        col = lax.broadcasted_iota(jnp.int32, (tq, win), 1)
        rel = col - half - row
        dist = jnp.abs(rel)
        in_band = dist <= half
        distf = (dist * dil).astype(F32)
        for variant in range(3):
            ok = in_band
            if variant == 0:
                ok = jnp.logical_and(ok, col >= half)
            if variant == 2:
                ok = jnp.logical_and(ok, col < half + tq)
            for h in range(N_HEADS):
                slope = 2.0 ** (-8.0 * (h + 1) / N_HEADS)
                rows = slice((h % 2) * tq, (h % 2 + 1) * tq)
                bias_ref[variant, h // 2, rows, :] = jnp.where(ok, -slope * distf, NEG)

    variant = jnp.where(qi == 0, 0, jnp.where(qi == n_q - 1, 2, 1))
    kwin = jnp.concatenate([kp_ref[...], kc_ref[...], kn_ref[...]], axis=0)
    vwin = jnp.concatenate([vp_ref[...], vc_ref[...], vn_ref[...]], axis=0)
    lane = lax.broadcasted_iota(jnp.int32, (1, lanes), 1)
    first_head = lane < HEAD_DIM
    lse_lane = lax.broadcasted_iota(jnp.int32, (tq, lanes), 1)
    lse_blk = jnp.zeros((tq, lanes), F32)
    zero = jnp.zeros((), BF16)
    scale = HEAD_DIM ** -0.5
    for hp in range(N_HEADS // 2):
        cols = slice(hp * lanes, (hp + 1) * lanes)
        q2 = q_ref[:, cols] * jnp.asarray(scale, BF16)
        q_st = jnp.concatenate([jnp.where(first_head, q2, zero),
                                jnp.where(first_head, zero, q2)], axis=0)
        s = lax.dot_general(q_st, kwin[:, cols], (((1,), (1,)), ((), ())),
                            preferred_element_type=F32)
        s = s + bias_ref[variant, hp]
        m = jnp.max(s, axis=-1, keepdims=True)
        p = jnp.exp(s - m)
        den = jnp.sum(p, axis=-1, keepdims=True)
        o_st = jnp.dot(p.astype(BF16), vwin[:, cols], preferred_element_type=F32)
        o_st = o_st * (1.0 / den)
        o_ref[:, cols] = jnp.where(first_head, o_st[:tq], o_st[tq:]).astype(BF16)
        lse = m + jnp.log(den)
        lse_blk = jnp.where(lse_lane == 2 * hp, lse[:tq], lse_blk)
        lse_blk = jnp.where(lse_lane == 2 * hp + 1, lse[tq:], lse_blk)
    lse_ref[...] = lse_blk


def _band_attention(qkv, dil, half, tq=128):
    _, n_seq, length, width = qkv.shape
    n_q = length // tq
    hb = tq // half
    n_half = length // half
    assert n_q >= 2

    def cur(t):
        return pl.BlockSpec((None, None, tq, width), lambda s, i: (t, s, i, 0))

    def prev(t):
        return pl.BlockSpec((None, None, half, width),
                            lambda s, i: (t, s, jnp.maximum(i * hb - 1, 0), 0))

    def nxt(t):
        return pl.BlockSpec((None, None, half, width),
                            lambda s, i: (t, s, jnp.minimum((i + 1) * hb, n_half - 1), 0))

    o_spec = pl.BlockSpec((None, tq, width), lambda s, i: (s, i, 0))
    lse_spec = pl.BlockSpec((None, tq, LANES), lambda s, i: (s, i, 0))
    kern = functools.partial(_attn_kernel, n_q=n_q, half=half, dil=dil)
    return pl.pallas_call(
        kern,
        grid=(n_seq, n_q),
        in_specs=[cur(0), prev(1), cur(1), nxt(1), prev(2), cur(2), nxt(2)],
        out_specs=[o_spec, lse_spec],
        out_shape=[jax.ShapeDtypeStruct((n_seq, length, width), BF16),
                   jax.ShapeDtypeStruct((n_seq, length, LANES), F32)],
        scratch_shapes=[pltpu.VMEM((3, N_HEADS // 2, 2 * tq, tq + 2 * half), F32)],
        compiler_params=_params(2),
        name=f"band_attn_d{dil}",
    )(qkv, qkv, qkv, qkv, qkv, qkv, qkv)


def _merge_kernel(o0_ref, o1_ref, o2_ref, l0_ref, l1_ref, l2_ref, x_ref, e_ref, wo_ref,
                  lng_ref, lnb_ref, unperm_ref, out_ref, lscr_ref, acc_ref, *, dils, strided):
    o_refs = (o0_ref, o1_ref, o2_ref)
    l_refs = (l0_ref, l1_ref, l2_ref)
    tm, d = x_ref.shape
    o_toks = []
    for g, dil in enumerate(dils):
        if dil == 1:
            o_toks.append(None)
        else:
            o_toks.append(jnp.dot(unperm_ref[strided.index(dil)], o_refs[g][...].reshape(tm, d),
                                  preferred_element_type=F32))
    lses = []
    for g, dil in enumerate(dils):
        if dil == 1:
            lses.append(l_refs[g][0])
            continue
        for r in range(dil):
            lscr_ref[g, pl.ds(r, tm // dil, stride=dil), :] = l_refs[g][r]
        lses.append(lscr_ref[g])
    top = jnp.maximum(jnp.maximum(lses[0], lses[1]), lses[2])
    ws = [jnp.exp(l - top) for l in lses]
    inv = 1.0 / (ws[0] + ws[1] + ws[2])
    for g, dil in enumerate(dils):
        w = ws[g] * inv
        w_hi = w.astype(BF16)
        w_lo = (w - w_hi.astype(F32)).astype(BF16)
        w_full = jnp.dot(jnp.concatenate([w_hi, w_lo], axis=1), e_ref[...],
                         preferred_element_type=F32)
        o_tok = o_refs[g][0].astype(F32) if dil == 1 else o_toks[g]
        if g == 0:
            acc_ref[...] = w_full * o_tok
        else:
            acc_ref[...] += w_full * o_tok
    mix = jnp.dot(acc_ref[...].astype(BF16), wo_ref[...], preferred_element_type=F32)
    z = DN_ALPHA * x_ref[...] + mix
    out_ref[...] = _layer_norm(z, lng_ref[...], lnb_ref[...])


def _merge_out_proj(outs, lses, x2d, seq, dils, w_o, ln_g, ln_b, tm=512):
    m, d = x2d.shape
    tiles_per_seq = seq // tm
    head_of_col = jnp.arange(d) // HEAD_DIM
    expand = (jnp.arange(LANES)[:, None] == head_of_col[None, :]).astype(BF16)
    expand = jnp.concatenate([expand, expand], axis=0)
    row = pl.BlockSpec((tm, d), lambda i: (i, 0))

    def classes(dil, width):
        return pl.BlockSpec((dil, tm // dil, width),
                            lambda i: (i // tiles_per_seq, i % tiles_per_seq, 0))

    strided = tuple(sorted({dil for dil in dils if dil != 1}))
    unperms = jnp.stack([_class_permutation(tm, dil).T for dil in strided])
    return pl.pallas_call(
        functools.partial(_merge_kernel, dils=dils, strided=strided),
        grid=(m // tm,),
        in_specs=([classes(dil, d) for dil in dils] + [classes(dil, LANES) for dil in dils]
                  + [row, _resident(expand.shape), _resident(w_o.shape),
                     _resident(ln_g.shape), _resident(ln_b.shape), _resident(unperms.shape)]),
        out_specs=row,
        out_shape=jax.ShapeDtypeStruct((m, d), F32),
        scratch_shapes=[pltpu.VMEM((len(dils), tm, LANES), F32),
                        pltpu.VMEM((tm, d), F32)],
        compiler_params=_params(1),
        name="merge_out_proj",
    )(*outs, *lses, x2d, expand, w_o, ln_g, ln_b, unperms)


def kernel(x, ln_g, ln_b, rg_w_in, rg_conv_w, rg_conv_b, rg_w_a, rg_b_a, rg_w_x, rg_b_x,
           rg_lam, rg_w_out, at_w_qkv, at_w_o, ff_w_up, ff_conv_w, ff_conv_b, ff_w_down):
    batch, seq, d = x.shape
    m = batch * seq
    d_rnn = rg_w_out.shape[1]

    def vec(p):
        return p.reshape(1, -1)

    ct, kw = MXU_TILE, 2 * MXU_TILE
    k_starts = _gate_k_starts(d_rnn, LRU_BLOCKS, ct, kw)
    wg_f = _gate_weights(rg_w_a[0, 0], rg_w_x[0, 0], k_starts, kw, ct)
    wg_b = _gate_weights(rg_w_a[0, 1], rg_w_x[0, 1], k_starts, kw, ct)
    gate, u, h_fwd = _in_proj_lru_fwd(
        x, rg_w_in[0].astype(BF16), rg_conv_w[0], vec(rg_conv_b[0]), wg_f,
        vec(0.5 * rg_b_a[0, 0]), vec(0.5 * rg_b_x[0, 0]), vec(rg_lam[0, 0]), k_starts,
        ct=ct, kw=kw)
    x1 = _lru_bwd(u, wg_b, vec(0.5 * rg_b_a[0, 1]), vec(0.5 * rg_b_x[0, 1]), vec(rg_lam[0, 1]),
                  h_fwd, gate, x, rg_w_out[0].astype(BF16), vec(ln_g[0, 0]), vec(ln_b[0, 0]),
                  k_starts, ct=ct, kw=kw)
    x2 = _conv_ffn(x1.reshape(m, d), seq, ff_w_up[0].astype(BF16), ff_conv_w[0],
                   vec(ff_conv_b[0]), ff_w_down[0].astype(BF16), vec(ln_g[0, 1]), vec(ln_b[0, 1]))

    dils = tuple(dil for _, dil in ATTN_GROUPS)
    qkv_groups = _qkv_proj(x2, at_w_qkv[0].astype(BF16), batch, seq, dils)
    outs, lses = [], []
    for qkv_g, (window, dil) in zip(qkv_groups, ATTN_GROUPS):
        o, lse = _band_attention(qkv_g, dil, window // (2 * dil))
        outs.append(o)
        lses.append(lse)
    x3 = _merge_out_proj(outs, lses, x2, seq, dils, at_w_o[0].astype(BF16),
                         vec(ln_g[1, 0]), vec(ln_b[1, 0]))
    x4 = _conv_ffn(x3, seq, ff_w_up[1].astype(BF16), ff_conv_w[1], vec(ff_conv_b[1]),
                   ff_w_down[1].astype(BF16), vec(ln_g[1, 1]), vec(ln_b[1, 1]))
    return x4.reshape(batch, seq, d)
```

```python
import functools
import math

import jax
import jax.numpy as jnp
from jax import lax
from jax.experimental import pallas as pl
from jax.experimental.pallas import tpu as pltpu

F32 = jnp.float32
BF16 = jnp.bfloat16

LRU_C = 8.0
LRU_BLOCKS = 16
ATTN_GROUPS = ((128, 1), (512, 4), (2048, 16))
N_HEADS = 16
HEAD_DIM = 64
LN_EPS = 1e-5
DEPTH = 2
DN_ALPHA = (2 * DEPTH) ** 0.25
NEG = -1e30

VMEM_LIMIT_BYTES = 56 * 1024 * 1024
MXU_TILE = 256
LANES = 128
BF16_ROWS = 16
F32_ROWS = 8

_GELU_C = math.sqrt(2.0 / math.pi)


def _gelu(x):
    inner = x * (_GELU_C + (_GELU_C * 0.044715) * (x * x))
    return x * (0.5 + 0.5 * jnp.tanh(inner))


def _layer_norm(z, g, b):
    mu = jnp.mean(z, axis=-1, keepdims=True)
    zc = z - mu
    var = jnp.mean(zc * zc, axis=-1, keepdims=True)
    return zc * lax.rsqrt(var + LN_EPS) * g + b


def _resident(shape):
    nd = len(shape)
    return pl.BlockSpec(shape, lambda *_: (0,) * nd, pipeline_mode=pl.Buffered(1))


def _params(n_axes):
    return pltpu.CompilerParams(
        dimension_semantics=("arbitrary",) * n_axes,
        vmem_limit_bytes=VMEM_LIMIT_BYTES)


def _fill_with_halo(xe_ref, x_prev_ref, x_ref, x_next_ref, i, tiles_per_seq):
    tm = x_ref.shape[0]
    halo = x_prev_ref.shape[0]
    first = (i % tiles_per_seq == 0)
    last = (i % tiles_per_seq == tiles_per_seq - 1)
    xe_ref[0:halo, :] = (x_prev_ref[...] * jnp.where(first, 0.0, 1.0)).astype(BF16)
    xe_ref[halo:halo + tm, :] = x_ref[...].astype(BF16)
    xe_ref[halo + tm:, :] = (x_next_ref[...] * jnp.where(last, 0.0, 1.0)).astype(BF16)


def _in_proj_kernel(x_prev_ref, x_ref, x_next_ref, w_ref, cw_ref, cb_ref, gate_ref, u_ref,
                    xe_ref, *, tiles_per_seq, d_rnn, cn):
    tm = x_ref.shape[0]
    halo = x_prev_ref.shape[0]
    n = tm + 2 * halo
    _fill_with_halo(xe_ref, x_prev_ref, x_ref, x_next_ref, pl.program_id(0), tiles_per_seq)
    def project(j):
        g = jnp.dot(xe_ref[halo:halo + tm, :], w_ref[:, j * cn:(j + 1) * cn],
                    preferred_element_type=F32)
        u = jnp.dot(xe_ref[...], w_ref[:, d_rnn + j * cn:d_rnn + (j + 1) * cn],
                    preferred_element_type=F32)
        return g, u

    n_chunks = d_rnn // cn
    ahead = project(0)
    for j in range(n_chunks):
        cols = slice(j * cn, (j + 1) * cn)
        g, u = ahead
        if j + 1 < n_chunks:
            ahead = project(j + 1)
        gate_ref[:, cols] = _gelu(g).astype(BF16)
        conv = (pltpu.roll(u, 2, 0) * cw_ref[0:1, cols] + pltpu.roll(u, 1, 0) * cw_ref[1:2, cols]
                + u * cw_ref[2:3, cols] + pltpu.roll(u, n - 1, 0) * cw_ref[3:4, cols]
                + cb_ref[:, cols])
        u_ref[:, cols] = conv[halo:halo + tm].astype(BF16)


def _halo_specs(m, d, tm, halo):
    hb = tm // halo
    n_halo = m // halo
    return [pl.BlockSpec((halo, d), lambda i: (jnp.maximum(i * hb - 1, 0), 0)),
            pl.BlockSpec((tm, d), lambda i: (i, 0)),
            pl.BlockSpec((halo, d), lambda i: (jnp.minimum((i + 1) * hb, n_halo - 1), 0))]


def _in_proj(x2d, seq, w_in, conv_w, conv_b, tm=512, cn=256):
    m, d = x2d.shape
    d_rnn = w_in.shape[1] // 2
    halo = BF16_ROWS
    kern = functools.partial(_in_proj_kernel, tiles_per_seq=seq // tm, d_rnn=d_rnn, cn=cn)
    return pl.pallas_call(
        kern,
        grid=(m // tm,),
        in_specs=_halo_specs(m, d, tm, halo) + [_resident(w_in.shape), _resident(conv_w.shape),
                                                _resident(conv_b.shape)],
        out_specs=[pl.BlockSpec((tm, d_rnn), lambda i: (i, 0)),
                   pl.BlockSpec((tm, d_rnn), lambda i: (i, 0))],
        out_shape=[jax.ShapeDtypeStruct((m, d_rnn), BF16),
                   jax.ShapeDtypeStruct((m, d_rnn), BF16)],
        scratch_shapes=[pltpu.VMEM((tm + 2 * halo, d), BF16)],
        compiler_params=_params(1),
        name="in_proj",
    )(x2d, x2d, x2d, w_in, conv_w, conv_b)


def _gate_k_starts(d_rnn, n_blocks, ct, kw):
    bw = d_rnn // n_blocks
    starts = []
    for j in range(d_rnn // ct):
        lo = (j * ct) // bw * bw
        hi = ((j * ct + ct - 1) // bw + 1) * bw
        k0 = min(lo // 128 * 128, d_rnn - kw)
        assert k0 <= lo and hi <= k0 + kw
        starts.append(k0)
    return tuple(starts)


def _scan_tile(a, b, h0, reverse):
    tt, cw = a.shape
    groups = tt // F32_ROWS
    a3 = a.reshape(groups, F32_ROWS, cw)
    b3 = b.reshape(groups, F32_ROWS, cw)
    row = lax.broadcasted_iota(jnp.int32, a3.shape, 1)
    for k in (1, 2, 4):
        if reverse:
            shift, keep = F32_ROWS - k, row < F32_ROWS - k
        else:
            shift, keep = k, row >= k
        ar = pltpu.roll(a3, shift, 1)
        br = pltpu.roll(b3, shift, 1)
        b3 = b3 + a3 * jnp.where(keep, br, 0.0)
        a3 = a3 * jnp.where(keep, ar, 1.0)
    out = [None] * groups
    h = h0
    for g in (reversed(range(groups)) if reverse else range(groups)):
        hg = b3[g] + a3[g] * h
        out[g] = hg
        h = hg[0:1] if reverse else hg[F32_ROWS - 1:F32_ROWS]
    return jnp.concatenate(out, axis=0), h


def _lru_tile(u_ref, wg_ref, ba_ref, bx_ref, lam_ref, carry_ref, emit, *, reverse, k_starts,
              kw, ct):
    @pl.when(pl.program_id(1) == 0)
    def _():
        carry_ref[...] = jnp.zeros_like(carry_ref)

    lam = lam_ref[...]
    neg_lam = -lam
    softplus = jnp.maximum(neg_lam, 0.0) + jnp.log1p(jnp.exp(-jnp.abs(neg_lam)))
    half_decay = (-0.5 * LRU_C) * softplus

    for j, k0 in enumerate(k_starts):
        cols = slice(j * ct, (j + 1) * ct)
        g = jnp.dot(u_ref[:, k0:k0 + kw], wg_ref[j], preferred_element_type=F32)
        tanh_r = jnp.tanh(g[:, :ct] + ba_ref[:, cols])
        i = 0.5 + 0.5 * jnp.tanh(g[:, ct:] + bx_ref[:, cols])
        log_a = half_decay[:, cols] + half_decay[:, cols] * tanh_r
        a = jnp.exp(log_a)
        gain_sq = -jnp.tanh(log_a) * (a * a + 1.0)
        gain = jnp.where(gain_sq > 0.0, gain_sq * lax.rsqrt(gain_sq), 0.0)
        b = gain * (i * u_ref[:, cols].astype(F32))
        h, h_last = _scan_tile(a, b, carry_ref[0:1, cols], reverse)
        carry_ref[0:1, cols] = h_last
        emit(j, cols, h)


def _lru_fwd_kernel(u_ref, wg_ref, ba_ref, bx_ref, lam_ref, h_ref, carry_ref, **kw):
    def emit(j, cols, h):
        h_ref[:, cols] = h.astype(BF16)

    _lru_tile(u_ref, wg_ref, ba_ref, bx_ref, lam_ref, carry_ref, emit, reverse=False, **kw)


def _lru_bwd_kernel(u_ref, wg_ref, ba_ref, bx_ref, lam_ref, hf_ref, gate_ref, x_ref, wout_ref,
                    lng_ref, lnb_ref, out_ref, carry_ref, y_ref, **kw):
    def emit(j, cols, h):
        y = gate_ref[:, cols].astype(F32) * (hf_ref[:, cols].astype(F32) + h)
        y_ref[:, cols] = y.astype(BF16)

    _lru_tile(u_ref, wg_ref, ba_ref, bx_ref, lam_ref, carry_ref, emit, reverse=True, **kw)
    mix = jnp.dot(y_ref[...], wout_ref[...], preferred_element_type=F32)
    z = DN_ALPHA * x_ref[...] + mix
    out_ref[...] = _layer_norm(z, lng_ref[...], lnb_ref[...])


def _lru_fwd(u, wg, b_a, b_x, lam, k_starts, tt=512, ct=256, kw=512):
    batch, seq, d_rnn = u.shape
    cur = pl.BlockSpec((None, tt, d_rnn), lambda b, i: (b, i, 0))
    kern = functools.partial(_lru_fwd_kernel, k_starts=k_starts, kw=kw, ct=ct)
    return pl.pallas_call(
        kern,
        grid=(batch, seq // tt),
        in_specs=[cur, _resident(wg.shape), _resident(b_a.shape), _resident(b_x.shape),
                  _resident(lam.shape)],
        out_specs=cur,
        out_shape=jax.ShapeDtypeStruct((batch, seq, d_rnn), BF16),
        scratch_shapes=[pltpu.VMEM((F32_ROWS, d_rnn), F32)],
        compiler_params=_params(2),
        name="lru_fwd",
    )(u, wg, b_a, b_x, lam)


def _lru_bwd(u, wg, b_a, b_x, lam, h_fwd, gate, x, w_out, ln_g, ln_b, k_starts,
             tt=512, ct=256, kw=512):
    batch, seq, d_rnn = u.shape
    d = x.shape[-1]
    n_tiles = seq // tt
    cur = pl.BlockSpec((None, tt, d_rnn), lambda b, i: (b, n_tiles - 1 - i, 0))
    xspec = pl.BlockSpec((None, tt, d), lambda b, i: (b, n_tiles - 1 - i, 0))
    kern = functools.partial(_lru_bwd_kernel, k_starts=k_starts, kw=kw, ct=ct)
    return pl.pallas_call(
        kern,
        grid=(batch, n_tiles),
        in_specs=[cur, _resident(wg.shape), _resident(b_a.shape), _resident(b_x.shape),
                  _resident(lam.shape), cur, cur, xspec, _resident(w_out.shape),
                  _resident(ln_g.shape), _resident(ln_b.shape)],
        out_specs=xspec,
        out_shape=jax.ShapeDtypeStruct((batch, seq, d), F32),
        scratch_shapes=[pltpu.VMEM((F32_ROWS, d_rnn), F32), pltpu.VMEM((tt, d_rnn), BF16)],
        compiler_params=_params(2),
        name="lru_bwd",
    )(u, wg, b_a, b_x, lam, h_fwd, gate, x, w_out, ln_g, ln_b)


def _gate_weights(w_a, w_x, k_starts, kw, ct):
    nb, bw, _ = w_a.shape
    d_rnn = nb * bw

    def dense(w):
        w = (0.5 * w).astype(BF16)
        return jnp.concatenate(
            [jnp.pad(w[n], ((0, 0), (n * bw, d_rnn - (n + 1) * bw))) for n in range(nb)], axis=0)

    wa, wx = dense(w_a), dense(w_x)
    tiles = [jnp.concatenate([wa[k0:k0 + kw, j * ct:(j + 1) * ct],
                              wx[k0:k0 + kw, j * ct:(j + 1) * ct]], axis=1)
             for j, k0 in enumerate(k_starts)]
    return jnp.stack(tiles).astype(BF16)


def _ffn_kernel(x_prev_ref, x_ref, x_next_ref, wup_ref, cw_ref, cb_ref, wdn_ref, lng_ref,
                lnb_ref, out_ref, xe_ref, acc_ref, hm_ref, *, tiles_per_seq, d_ff, cf,
                down_every):
    tm = x_ref.shape[0]
    halo = x_prev_ref.shape[0]
    _fill_with_halo(xe_ref, x_prev_ref, x_ref, x_next_ref, pl.program_id(0), tiles_per_seq)
    n = tm + 2 * halo
    n_chunks = d_ff // cf

    def down(lo, hi):
        part = jnp.dot(hm_ref[:, lo * cf:hi * cf], wdn_ref[lo * cf:hi * cf, :],
                       preferred_element_type=F32)
        if lo == 0:
            acc_ref[...] = part
        else:
            acc_ref[...] += part

    pending = None
    start = 0
    for c in range(n_chunks):
        cols = slice(c * cf, (c + 1) * cf)
        g = jnp.dot(xe_ref[...], wup_ref[:, d_ff + c * cf:d_ff + (c + 1) * cf],
                    preferred_element_type=F32)
        v = jnp.dot(xe_ref[halo:halo + tm, :], wup_ref[:, cols], preferred_element_type=F32)
        if pending is not None:
            down(*pending)
            pending = None
        gc = (pltpu.roll(g, 1, 0) * cw_ref[0:1, cols] + g * cw_ref[1:2, cols]
              + pltpu.roll(g, n - 1, 0) * cw_ref[2:3, cols] + cb_ref[:, cols])[halo:halo + tm]
        hm_ref[:, cols] = (_gelu(gc) * v).astype(BF16)
        if (c + 1) % down_every == 0 or c + 1 == n_chunks:
            pending = (start, c + 1)
            start = c + 1
    down(*pending)
    z = DN_ALPHA * x_ref[...] + acc_ref[...]
    out_ref[...] = _layer_norm(z, lng_ref[...], lnb_ref[...])


def _conv_ffn(x2d, seq, w_up, conv_w, conv_b, w_down, ln_g, ln_b, tm=512, cf=256,
              down_every=6):
    m, d = x2d.shape
    d_ff = w_down.shape[0]
    halo = BF16_ROWS
    kern = functools.partial(_ffn_kernel, tiles_per_seq=seq // tm, d_ff=d_ff, cf=cf,
                             down_every=down_every)
    return pl.pallas_call(
        kern,
        grid=(m // tm,),
        in_specs=_halo_specs(m, d, tm, halo) + [
            _resident(w_up.shape), _resident(conv_w.shape), _resident(conv_b.shape),
            _resident(w_down.shape), _resident(ln_g.shape), _resident(ln_b.shape)],
        out_specs=pl.BlockSpec((tm, d), lambda i: (i, 0)),
        out_shape=jax.ShapeDtypeStruct((m, d), F32),
        scratch_shapes=[pltpu.VMEM((tm + 2 * halo, d), BF16), pltpu.VMEM((tm, d), F32),
                        pltpu.VMEM((tm, d_ff), BF16)],
        compiler_params=_params(1),
        name="conv_ffn",
    )(x2d, x2d, x2d, w_up, conv_w, conv_b, w_down, ln_g, ln_b)


def _class_permutation(tm, dil):
    out_row = jnp.arange(tm)
    src = (out_row % (tm // dil)) * dil + out_row // (tm // dil)
    return (src[:, None] == jnp.arange(tm)[None, :]).astype(BF16)


def _qkv_kernel(x_ref, w_ref, perm_ref, o0_ref, o1_ref, o2_ref, xb_ref, *, cn, dils, strided,
                width):
    out_refs = (o0_ref, o1_ref, o2_ref)
    tm = x_ref.shape[0]
    xb_ref[0] = x_ref[...].astype(BF16)
    lhs_of = {1: 0}
    for slot, dil in enumerate(strided, start=1):
        xb_ref[slot] = jnp.dot(perm_ref[slot - 1], xb_ref[0],
                               preferred_element_type=F32).astype(BF16)
        lhs_of[dil] = slot
    for j in range(w_ref.shape[1] // cn):
        c0 = j * cn
        g, t, off = c0 // (3 * width), (c0 % (3 * width)) // width, c0 % width
        dil = dils[g]
        res = jnp.dot(xb_ref[lhs_of[dil]], w_ref[:, c0:c0 + cn], preferred_element_type=F32)
        rows = tm // dil
        for r in range(dil):
            out_refs[g][t, r, :, off:off + cn] = res[r * rows:(r + 1) * rows].astype(BF16)


def _qkv_proj(x2d, w_qkv, batch, seq, dils, tm=512, cn=512):
    m, d = x2d.shape
    width = w_qkv.shape[1] // (3 * len(dils))
    tiles_per_seq = seq // tm
    strided = tuple(sorted({dil for dil in dils if dil != 1}))
    perms = jnp.stack([_class_permutation(tm, dil) for dil in strided])
    out_specs = [pl.BlockSpec((3, dil, tm // dil, width),
                              lambda i: (0, i // tiles_per_seq, i % tiles_per_seq, 0))
                 for dil in dils]
    out_shape = [jax.ShapeDtypeStruct((3, batch * dil, seq // dil, width), BF16) for dil in dils]
    return pl.pallas_call(
        functools.partial(_qkv_kernel, cn=cn, dils=dils, strided=strided, width=width),
        grid=(m // tm,),
        in_specs=[pl.BlockSpec((tm, d), lambda i: (i, 0)), _resident(w_qkv.shape),
                  _resident(perms.shape)],
        out_specs=out_specs,
        out_shape=out_shape,
        scratch_shapes=[pltpu.VMEM((1 + len(strided), tm, d), BF16)],
        compiler_params=_params(1),
        name="qkv_proj",
    )(x2d, w_qkv, perms)


def _attn_kernel(q_ref, kp_ref, kc_ref, kn_ref, vp_ref, vc_ref, vn_ref, o_ref, lse_ref,
                 bias_ref, *, n_q, half, dil):
    qi = pl.program_id(1)
    tq = q_ref.shape[0]
    win = tq + 2 * half
    lanes = 2 * HEAD_DIM

    @pl.when(jnp.logical_and(pl.program_id(0) == 0, qi == 0))
    def _():
        row = lax.broadcasted_iota(jnp.int32, (tq, win), 0)
        col = lax.broadcasted_iota(jnp.int32, (tq, win), 1)
        rel = col - half - row
        dist = jnp.abs(rel)
        in_band = dist <= half
        distf = (dist * dil).astype(F32)
        for variant in range(3):
            ok = in_band
            if variant == 0:
                ok = jnp.logical_and(ok, col >= half)
            if variant == 2:
                ok = jnp.logical_and(ok, col < half + tq)
            for h in range(N_HEADS):
                slope = 2.0 ** (-8.0 * (h + 1) / N_HEADS)
                rows = slice((h % 2) * tq, (h % 2 + 1) * tq)
                bias_ref[variant, h // 2, rows, :] = jnp.where(ok, -slope * distf, NEG)

    variant = jnp.where(qi == 0, 0, jnp.where(qi == n_q - 1, 2, 1))
    kwin = jnp.concatenate([kp_ref[...], kc_ref[...], kn_ref[...]], axis=0)
    vwin = jnp.concatenate([vp_ref[...], vc_ref[...], vn_ref[...]], axis=0)
    lane = lax.broadcasted_iota(jnp.int32, (1, lanes), 1)
    first_head = lane < HEAD_DIM
    lse_lane = lax.broadcasted_iota(jnp.int32, (tq, lanes), 1)
    lse_blk = jnp.zeros((tq, lanes), F32)
    zero = jnp.zeros((), BF16)
    scale = HEAD_DIM ** -0.5
    for hp in range(N_HEADS // 2):
        cols = slice(hp * lanes, (hp + 1) * lanes)
        q2 = q_ref[:, cols] * jnp.asarray(scale, BF16)
        q_st = jnp.concatenate([jnp.where(first_head, q2, zero),
                                jnp.where(first_head, zero, q2)], axis=0)
        s = lax.dot_general(q_st, kwin[:, cols], (((1,), (1,)), ((), ())),
                            preferred_element_type=F32)
        s = s + bias_ref[variant, hp]
        m = jnp.max(s, axis=-1, keepdims=True)
        p = jnp.exp(s - m)
        den = jnp.sum(p, axis=-1, keepdims=True)
        o_st = jnp.dot(p.astype(BF16), vwin[:, cols], preferred_element_type=F32)
        o_st = o_st * (1.0 / den)
        o_ref[:, cols] = jnp.where(first_head, o_st[:tq], o_st[tq:]).astype(BF16)
        lse = m + jnp.log(den)
        lse_blk = jnp.where(lse_lane == 2 * hp, lse[:tq], lse_blk)
        lse_blk = jnp.where(lse_lane == 2 * hp + 1, lse[tq:], lse_blk)
    lse_ref[...] = lse_blk


def _band_attention(qkv, dil, half, tq=128):
    _, n_seq, length, width = qkv.shape
    n_q = length // tq
    hb = tq // half
    n_half = length // half
    assert n_q >= 2

    def cur(t):
        return pl.BlockSpec((None, None, tq, width), lambda s, i: (t, s, i, 0))

    def prev(t):
        return pl.BlockSpec((None, None, half, width),
                            lambda s, i: (t, s, jnp.maximum(i * hb - 1, 0), 0))

    def nxt(t):
        return pl.BlockSpec((None, None, half, width),
                            lambda s, i: (t, s, jnp.minimum((i + 1) * hb, n_half - 1), 0))

    o_spec = pl.BlockSpec((None, tq, width), lambda s, i: (s, i, 0))
    lse_spec = pl.BlockSpec((None, tq, LANES), lambda s, i: (s, i, 0))
    kern = functools.partial(_attn_kernel, n_q=n_q, half=half, dil=dil)
    return pl.pallas_call(
        kern,
        grid=(n_seq, n_q),
        in_specs=[cur(0), prev(1), cur(1), nxt(1), prev(2), cur(2), nxt(2)],
        out_specs=[o_spec, lse_spec],
        out_shape=[jax.ShapeDtypeStruct((n_seq, length, width), BF16),
                   jax.ShapeDtypeStruct((n_seq, length, LANES), F32)],
        scratch_shapes=[pltpu.VMEM((3, N_HEADS // 2, 2 * tq, tq + 2 * half), F32)],
        compiler_params=_params(2),
        name=f"band_attn_d{dil}",
    )(qkv, qkv, qkv, qkv, qkv, qkv, qkv)


def _merge_kernel(o0_ref, o1_ref, o2_ref, l0_ref, l1_ref, l2_ref, x_ref, e_ref, wo_ref,
                  lng_ref, lnb_ref, unperm_ref, out_ref, lscr_ref, acc_ref, *, dils, strided):
    o_refs = (o0_ref, o1_ref, o2_ref)
    l_refs = (l0_ref, l1_ref, l2_ref)
    tm, d = x_ref.shape
    o_toks = []
    for g, dil in enumerate(dils):
        if dil == 1:
            o_toks.append(None)
        else:
            o_toks.append(jnp.dot(unperm_ref[strided.index(dil)], o_refs[g][...].reshape(tm, d),
                                  preferred_element_type=F32))
    lses = []
    for g, dil in enumerate(dils):
        if dil == 1:
            lses.append(l_refs[g][0])
            continue
        for r in range(dil):
            lscr_ref[g, pl.ds(r, tm // dil, stride=dil), :] = l_refs[g][r]
        lses.append(lscr_ref[g])
    top = jnp.maximum(jnp.maximum(lses[0], lses[1]), lses[2])
    ws = [jnp.exp(l - top) for l in lses]
    inv = 1.0 / (ws[0] + ws[1] + ws[2])
    for g, dil in enumerate(dils):
        w = ws[g] * inv
        w_hi = w.astype(BF16)
        w_lo = (w - w_hi.astype(F32)).astype(BF16)
        w_full = jnp.dot(jnp.concatenate([w_hi, w_lo], axis=1), e_ref[...],
                         preferred_element_type=F32)
        o_tok = o_refs[g][0].astype(F32) if dil == 1 else o_toks[g]
        if g == 0:
            acc_ref[...] = w_full * o_tok
        else:
            acc_ref[...] += w_full * o_tok
    mix = jnp.dot(acc_ref[...].astype(BF16), wo_ref[...], preferred_element_type=F32)
    z = DN_ALPHA * x_ref[...] + mix
    out_ref[...] = _layer_norm(z, lng_ref[...], lnb_ref[...])


def _merge_out_proj(outs, lses, x2d, seq, dils, w_o, ln_g, ln_b, tm=512):
    m, d = x2d.shape
    tiles_per_seq = seq // tm
    head_of_col = jnp.arange(d) // HEAD_DIM
    expand = (jnp.arange(LANES)[:, None] == head_of_col[None, :]).astype(BF16)
    expand = jnp.concatenate([expand, expand], axis=0)
    row = pl.BlockSpec((tm, d), lambda i: (i, 0))

    def classes(dil, width):
        return pl.BlockSpec((dil, tm // dil, width),
                            lambda i: (i // tiles_per_seq, i % tiles_per_seq, 0))

    strided = tuple(sorted({dil for dil in dils if dil != 1}))
    unperms = jnp.stack([_class_permutation(tm, dil).T for dil in strided])
    return pl.pallas_call(
        functools.partial(_merge_kernel, dils=dils, strided=strided),
        grid=(m // tm,),
        in_specs=([classes(dil, d) for dil in dils] + [classes(dil, LANES) for dil in dils]
                  + [row, _resident(expand.shape), _resident(w_o.shape),
                     _resident(ln_g.shape), _resident(ln_b.shape), _resident(unperms.shape)]),
        out_specs=row,
        out_shape=jax.ShapeDtypeStruct((m, d), F32),
        scratch_shapes=[pltpu.VMEM((len(dils), tm, LANES), F32),
                        pltpu.VMEM((tm, d), F32)],
        compiler_params=_params(1),
        name="merge_out_proj",
    )(*outs, *lses, x2d, expand, w_o, ln_g, ln_b, unperms)


def kernel(x, ln_g, ln_b, rg_w_in, rg_conv_w, rg_conv_b, rg_w_a, rg_b_a, rg_w_x, rg_b_x,
           rg_lam, rg_w_out, at_w_qkv, at_w_o, ff_w_up, ff_conv_w, ff_conv_b, ff_w_down):
    batch, seq, d = x.shape
    m = batch * seq
    d_rnn = rg_w_out.shape[1]

    def vec(p):
        return p.reshape(1, -1)

    ct, kw = MXU_TILE, 2 * MXU_TILE
    k_starts = _gate_k_starts(d_rnn, LRU_BLOCKS, ct, kw)
    gate, u = _in_proj(x.reshape(m, d), seq, rg_w_in[0].astype(BF16), rg_conv_w[0],
                       vec(rg_conv_b[0]))
    gate = gate.reshape(batch, seq, d_rnn)
    u = u.reshape(batch, seq, d_rnn)
    wg_f = _gate_weights(rg_w_a[0, 0], rg_w_x[0, 0], k_starts, kw, ct)
    wg_b = _gate_weights(rg_w_a[0, 1], rg_w_x[0, 1], k_starts, kw, ct)
    h_fwd = _lru_fwd(u, wg_f, vec(0.5 * rg_b_a[0, 0]), vec(0.5 * rg_b_x[0, 0]), vec(rg_lam[0, 0]),
                     k_starts, ct=ct, kw=kw)
    x1 = _lru_bwd(u, wg_b, vec(0.5 * rg_b_a[0, 1]), vec(0.5 * rg_b_x[0, 1]), vec(rg_lam[0, 1]),
                  h_fwd, gate, x, rg_w_out[0].astype(BF16), vec(ln_g[0, 0]), vec(ln_b[0, 0]),
                  k_starts, ct=ct, kw=kw)
    x2 = _conv_ffn(x1.reshape(m, d), seq, ff_w_up[0].astype(BF16), ff_conv_w[0],
                   vec(ff_conv_b[0]), ff_w_down[0].astype(BF16), vec(ln_g[0, 1]), vec(ln_b[0, 1]))

    dils = tuple(dil for _, dil in ATTN_GROUPS)
    qkv_groups = _qkv_proj(x2, at_w_qkv[0].astype(BF16), batch, seq, dils)
    outs, lses = [], []
    for qkv_g, (window, dil) in zip(qkv_groups, ATTN_GROUPS):
        o, lse = _band_attention(qkv_g, dil, window // (2 * dil))
        outs.append(o)
        lses.append(lse)
    x3 = _merge_out_proj(outs, lses, x2, seq, dils, at_w_o[0].astype(BF16),
                         vec(ln_g[1, 0]), vec(ln_b[1, 0]))
    x4 = _conv_ffn(x3, seq, ff_w_up[1].astype(BF16), ff_conv_w[1], vec(ff_conv_b[1]),
                   ff_w_down[1].astype(BF16), vec(ln_g[1, 1]), vec(ln_b[1, 1]))
    return x4.reshape(batch, seq, d)
```

```python
import functools
import math

import jax
import jax.numpy as jnp
from jax import lax
from jax.experimental import pallas as pl
from jax.experimental.pallas import tpu as pltpu

F32 = jnp.float32
BF16 = jnp.bfloat16

LRU_C = 8.0
LRU_BLOCKS = 16
ATTN_GROUPS = ((128, 1), (512, 4), (2048, 16))
N_HEADS = 16
HEAD_DIM = 64
LN_EPS = 1e-5
DEPTH = 2
DN_ALPHA = (2 * DEPTH) ** 0.25
NEG = -1e30

VMEM_LIMIT_BYTES = 56 * 1024 * 1024
MXU_TILE = 256
LANES = 128
BF16_ROWS = 16
F32_ROWS = 8

_GELU_C = math.sqrt(2.0 / math.pi)


def _gelu(x):
    inner = x * (_GELU_C + (_GELU_C * 0.044715) * (x * x))
    return x * (0.5 + 0.5 * jnp.tanh(inner))


def _layer_norm(z, g, b):
    mu = jnp.mean(z, axis=-1, keepdims=True)
    zc = z - mu
    var = jnp.mean(zc * zc, axis=-1, keepdims=True)
    return zc * lax.rsqrt(var + LN_EPS) * g + b


def _resident(shape):
    nd = len(shape)
    return pl.BlockSpec(shape, lambda *_: (0,) * nd, pipeline_mode=pl.Buffered(1))


def _params(n_axes):
    return pltpu.CompilerParams(
        dimension_semantics=("arbitrary",) * n_axes,
        vmem_limit_bytes=VMEM_LIMIT_BYTES)


def _fill_with_halo(xe_ref, x_prev_ref, x_ref, x_next_ref, i, tiles_per_seq):
    tm = x_ref.shape[0]
    halo = x_prev_ref.shape[0]
    first = (i % tiles_per_seq == 0)
    last = (i % tiles_per_seq == tiles_per_seq - 1)
    xe_ref[0:halo, :] = (x_prev_ref[...] * jnp.where(first, 0.0, 1.0)).astype(BF16)
    xe_ref[halo:halo + tm, :] = x_ref[...].astype(BF16)
    xe_ref[halo + tm:, :] = (x_next_ref[...] * jnp.where(last, 0.0, 1.0)).astype(BF16)


def _in_proj_kernel(x_prev_ref, x_ref, x_next_ref, w_ref, cw_ref, cb_ref, gate_ref, u_ref,
                    xe_ref, *, tiles_per_seq, d_rnn, cn):
    tm = x_ref.shape[0]
    halo = x_prev_ref.shape[0]
    n = tm + 2 * halo
    _fill_with_halo(xe_ref, x_prev_ref, x_ref, x_next_ref, pl.program_id(0), tiles_per_seq)
    def project(j):
        g = jnp.dot(xe_ref[halo:halo + tm, :], w_ref[:, j * cn:(j + 1) * cn],
                    preferred_element_type=F32)
        u = jnp.dot(xe_ref[...], w_ref[:, d_rnn + j * cn:d_rnn + (j + 1) * cn],
                    preferred_element_type=F32)
        return g, u

    n_chunks = d_rnn // cn
    ahead = project(0)
    for j in range(n_chunks):
        cols = slice(j * cn, (j + 1) * cn)
        g, u = ahead
        if j + 1 < n_chunks:
            ahead = project(j + 1)
        gate_ref[:, cols] = _gelu(g).astype(BF16)
        conv = (pltpu.roll(u, 2, 0) * cw_ref[0:1, cols] + pltpu.roll(u, 1, 0) * cw_ref[1:2, cols]
                + u * cw_ref[2:3, cols] + pltpu.roll(u, n - 1, 0) * cw_ref[3:4, cols]
                + cb_ref[:, cols])
        u_ref[:, cols] = conv[halo:halo + tm].astype(BF16)


def _halo_specs(m, d, tm, halo):
    hb = tm // halo
    n_halo = m // halo
    return [pl.BlockSpec((halo, d), lambda i: (jnp.maximum(i * hb - 1, 0), 0)),
            pl.BlockSpec((tm, d), lambda i: (i, 0)),
            pl.BlockSpec((halo, d), lambda i: (jnp.minimum((i + 1) * hb, n_halo - 1), 0))]


def _in_proj(x2d, seq, w_in, conv_w, conv_b, tm=512, cn=256):
    m, d = x2d.shape
    d_rnn = w_in.shape[1] // 2
    halo = BF16_ROWS
    kern = functools.partial(_in_proj_kernel, tiles_per_seq=seq // tm, d_rnn=d_rnn, cn=cn)
    return pl.pallas_call(
        kern,
        grid=(m // tm,),
        in_specs=_halo_specs(m, d, tm, halo) + [_resident(w_in.shape), _resident(conv_w.shape),
                                                _resident(conv_b.shape)],
        out_specs=[pl.BlockSpec((tm, d_rnn), lambda i: (i, 0)),
                   pl.BlockSpec((tm, d_rnn), lambda i: (i, 0))],
        out_shape=[jax.ShapeDtypeStruct((m, d_rnn), BF16),
                   jax.ShapeDtypeStruct((m, d_rnn), BF16)],
        scratch_shapes=[pltpu.VMEM((tm + 2 * halo, d), BF16)],
        compiler_params=_params(1),
        name="in_proj",
    )(x2d, x2d, x2d, w_in, conv_w, conv_b)


def _gate_k_starts(d_rnn, n_blocks, ct, kw):
    bw = d_rnn // n_blocks
    starts = []
    for j in range(d_rnn // ct):
        lo = (j * ct) // bw * bw
        hi = ((j * ct + ct - 1) // bw + 1) * bw
        k0 = min(lo // 128 * 128, d_rnn - kw)
        assert k0 <= lo and hi <= k0 + kw
        starts.append(k0)
    return tuple(starts)


def _scan_tile(a, b, h0, reverse):
    tt, cw = a.shape
    groups = tt // F32_ROWS
    a3 = a.reshape(groups, F32_ROWS, cw)
    b3 = b.reshape(groups, F32_ROWS, cw)
    row = lax.broadcasted_iota(jnp.int32, a3.shape, 1)
    for k in (1, 2, 4):
        if reverse:
            shift, keep = F32_ROWS - k, row < F32_ROWS - k
        else:
            shift, keep = k, row >= k
        ar = pltpu.roll(a3, shift, 1)
        br = pltpu.roll(b3, shift, 1)
        b3 = b3 + a3 * jnp.where(keep, br, 0.0)
        a3 = a3 * jnp.where(keep, ar, 1.0)
    out = [None] * groups
    h = h0
    for g in (reversed(range(groups)) if reverse else range(groups)):
        hg = b3[g] + a3[g] * h
        out[g] = hg
        h = hg[0:1] if reverse else hg[F32_ROWS - 1:F32_ROWS]
    return jnp.concatenate(out, axis=0), h


def _lru_tile(u_ref, wg_ref, ba_ref, bx_ref, lam_ref, carry_ref, emit, *, reverse, k_starts,
              kw, ct):
    @pl.when(pl.program_id(1) == 0)
    def _():
        carry_ref[...] = jnp.zeros_like(carry_ref)

    lam = lam_ref[...]
    neg_lam = -lam
    softplus = jnp.maximum(neg_lam, 0.0) + jnp.log1p(jnp.exp(-jnp.abs(neg_lam)))
    half_decay = (-0.5 * LRU_C) * softplus

    for j, k0 in enumerate(k_starts):
        cols = slice(j * ct, (j + 1) * ct)
        g = jnp.dot(u_ref[:, k0:k0 + kw], wg_ref[j], preferred_element_type=F32)
        tanh_r = jnp.tanh(g[:, :ct] + ba_ref[:, cols])
        i = 0.5 + 0.5 * jnp.tanh(g[:, ct:] + bx_ref[:, cols])
        log_a = half_decay[:, cols] + half_decay[:, cols] * tanh_r
        a = jnp.exp(log_a)
        gain_sq = -jnp.tanh(log_a) * (a * a + 1.0)
        gain = jnp.where(gain_sq > 0.0, gain_sq * lax.rsqrt(gain_sq), 0.0)
        b = gain * (i * u_ref[:, cols].astype(F32))
        h, h_last = _scan_tile(a, b, carry_ref[0:1, cols], reverse)
        carry_ref[0:1, cols] = h_last
        emit(j, cols, h)


def _lru_fwd_kernel(u_ref, wg_ref, ba_ref, bx_ref, lam_ref, h_ref, carry_ref, **kw):
    def emit(j, cols, h):
        h_ref[:, cols] = h.astype(BF16)

    _lru_tile(u_ref, wg_ref, ba_ref, bx_ref, lam_ref, carry_ref, emit, reverse=False, **kw)


def _lru_bwd_kernel(u_ref, wg_ref, ba_ref, bx_ref, lam_ref, hf_ref, gate_ref, x_ref, wout_ref,
                    lng_ref, lnb_ref, out_ref, carry_ref, y_ref, **kw):
    def emit(j, cols, h):
        y = gate_ref[:, cols].astype(F32) * (hf_ref[:, cols].astype(F32) + h)
        y_ref[:, cols] = y.astype(BF16)

    _lru_tile(u_ref, wg_ref, ba_ref, bx_ref, lam_ref, carry_ref, emit, reverse=True, **kw)
    mix = jnp.dot(y_ref[...], wout_ref[...], preferred_element_type=F32)
    z = DN_ALPHA * x_ref[...] + mix
    out_ref[...] = _layer_norm(z, lng_ref[...], lnb_ref[...])


def _lru_fwd(u, wg, b_a, b_x, lam, k_starts, tt=1024, ct=256, kw=512):
    batch, seq, d_rnn = u.shape
    cur = pl.BlockSpec((None, tt, d_rnn), lambda b, i: (b, i, 0))
    kern = functools.partial(_lru_fwd_kernel, k_starts=k_starts, kw=kw, ct=ct)
    return pl.pallas_call(
        kern,
        grid=(batch, seq // tt),
        in_specs=[cur, _resident(wg.shape), _resident(b_a.shape), _resident(b_x.shape),
                  _resident(lam.shape)],
        out_specs=cur,
        out_shape=jax.ShapeDtypeStruct((batch, seq, d_rnn), BF16),
        scratch_shapes=[pltpu.VMEM((F32_ROWS, d_rnn), F32)],
        compiler_params=_params(2),
        name="lru_fwd",
    )(u, wg, b_a, b_x, lam)


def _lru_bwd(u, wg, b_a, b_x, lam, h_fwd, gate, x, w_out, ln_g, ln_b, k_starts,
             tt=1024, ct=256, kw=512):
    batch, seq, d_rnn = u.shape
    d = x.shape[-1]
    n_tiles = seq // tt
    cur = pl.BlockSpec((None, tt, d_rnn), lambda b, i: (b, n_tiles - 1 - i, 0))
    xspec = pl.BlockSpec((None, tt, d), lambda b, i: (b, n_tiles - 1 - i, 0))
    kern = functools.partial(_lru_bwd_kernel, k_starts=k_starts, kw=kw, ct=ct)
    return pl.pallas_call(
        kern,
        grid=(batch, n_tiles),
        in_specs=[cur, _resident(wg.shape), _resident(b_a.shape), _resident(b_x.shape),
                  _resident(lam.shape), cur, cur, xspec, _resident(w_out.shape),
                  _resident(ln_g.shape), _resident(ln_b.shape)],
        out_specs=xspec,
        out_shape=jax.ShapeDtypeStruct((batch, seq, d), F32),
        scratch_shapes=[pltpu.VMEM((F32_ROWS, d_rnn), F32), pltpu.VMEM((tt, d_rnn), BF16)],
        compiler_params=_params(2),
        name="lru_bwd",
    )(u, wg, b_a, b_x, lam, h_fwd, gate, x, w_out, ln_g, ln_b)


def _gate_weights(w_a, w_x, k_starts, kw, ct):
    nb, bw, _ = w_a.shape
    d_rnn = nb * bw

    def dense(w):
        w = (0.5 * w).astype(BF16)
        return jnp.concatenate(
            [jnp.pad(w[n], ((0, 0), (n * bw, d_rnn - (n + 1) * bw))) for n in range(nb)], axis=0)

    wa, wx = dense(w_a), dense(w_x)
    tiles = [jnp.concatenate([wa[k0:k0 + kw, j * ct:(j + 1) * ct],
                              wx[k0:k0 + kw, j * ct:(j + 1) * ct]], axis=1)
             for j, k0 in enumerate(k_starts)]
    return jnp.stack(tiles).astype(BF16)


def _ffn_kernel(x_prev_ref, x_ref, x_next_ref, wup_ref, cw_ref, cb_ref, wdn_ref, lng_ref,
                lnb_ref, out_ref, xe_ref, acc_ref, hm_ref, *, tiles_per_seq, d_ff, cf,
                down_every):
    tm = x_ref.shape[0]
    halo = x_prev_ref.shape[0]
    _fill_with_halo(xe_ref, x_prev_ref, x_ref, x_next_ref, pl.program_id(0), tiles_per_seq)
    n = tm + 2 * halo
    n_chunks = d_ff // cf

    def down(lo, hi):
        part = jnp.dot(hm_ref[:, lo * cf:hi * cf], wdn_ref[lo * cf:hi * cf, :],
                       preferred_element_type=F32)
        if lo == 0:
            acc_ref[...] = part
        else:
            acc_ref[...] += part

    pending = None
    start = 0
    for c in range(n_chunks):
        cols = slice(c * cf, (c + 1) * cf)
        g = jnp.dot(xe_ref[...], wup_ref[:, d_ff + c * cf:d_ff + (c + 1) * cf],
                    preferred_element_type=F32)
        v = jnp.dot(xe_ref[halo:halo + tm, :], wup_ref[:, cols], preferred_element_type=F32)
        if pending is not None:
            down(*pending)
            pending = None
        gc = (pltpu.roll(g, 1, 0) * cw_ref[0:1, cols] + g * cw_ref[1:2, cols]
              + pltpu.roll(g, n - 1, 0) * cw_ref[2:3, cols] + cb_ref[:, cols])[halo:halo + tm]
        hm_ref[:, cols] = (_gelu(gc) * v).astype(BF16)
        if (c + 1) % down_every == 0 or c + 1 == n_chunks:
            pending = (start, c + 1)
            start = c + 1
    down(*pending)
    z = DN_ALPHA * x_ref[...] + acc_ref[...]
    out_ref[...] = _layer_norm(z, lng_ref[...], lnb_ref[...])


def _conv_ffn(x2d, seq, w_up, conv_w, conv_b, w_down, ln_g, ln_b, tm=512, cf=256,
              down_every=6):
    m, d = x2d.shape
    d_ff = w_down.shape[0]
    halo = BF16_ROWS
    kern = functools.partial(_ffn_kernel, tiles_per_seq=seq // tm, d_ff=d_ff, cf=cf,
                             down_every=down_every)
    return pl.pallas_call(
        kern,
        grid=(m // tm,),
        in_specs=_halo_specs(m, d, tm, halo) + [
            _resident(w_up.shape), _resident(conv_w.shape), _resident(conv_b.shape),
            _resident(w_down.shape), _resident(ln_g.shape), _resident(ln_b.shape)],
        out_specs=pl.BlockSpec((tm, d), lambda i: (i, 0)),
        out_shape=jax.ShapeDtypeStruct((m, d), F32),
        scratch_shapes=[pltpu.VMEM((tm + 2 * halo, d), BF16), pltpu.VMEM((tm, d), F32),
                        pltpu.VMEM((tm, d_ff), BF16)],
        compiler_params=_params(1),
        name="conv_ffn",
    )(x2d, x2d, x2d, w_up, conv_w, conv_b, w_down, ln_g, ln_b)


def _class_permutation(tm, dil):
    out_row = jnp.arange(tm)
    src = (out_row % (tm // dil)) * dil + out_row // (tm // dil)
    return (src[:, None] == jnp.arange(tm)[None, :]).astype(BF16)


def _qkv_kernel(x_ref, w_ref, perm_ref, o0_ref, o1_ref, o2_ref, xb_ref, *, cn, dils, strided,
                width):
    out_refs = (o0_ref, o1_ref, o2_ref)
    tm = x_ref.shape[0]
    xb_ref[0] = x_ref[...].astype(BF16)
    lhs_of = {1: 0}
    for slot, dil in enumerate(strided, start=1):
        xb_ref[slot] = jnp.dot(perm_ref[slot - 1], xb_ref[0],
                               preferred_element_type=F32).astype(BF16)
        lhs_of[dil] = slot
    for j in range(w_ref.shape[1] // cn):
        c0 = j * cn
        g, t, off = c0 // (3 * width), (c0 % (3 * width)) // width, c0 % width
        dil = dils[g]
        res = jnp.dot(xb_ref[lhs_of[dil]], w_ref[:, c0:c0 + cn], preferred_element_type=F32)
        rows = tm // dil
        for r in range(dil):
            out_refs[g][t, r, :, off:off + cn] = res[r * rows:(r + 1) * rows].astype(BF16)


def _qkv_proj(x2d, w_qkv, batch, seq, dils, tm=512, cn=512):
    m, d = x2d.shape
    width = w_qkv.shape[1] // (3 * len(dils))
    tiles_per_seq = seq // tm
    strided = tuple(sorted({dil for dil in dils if dil != 1}))
    perms = jnp.stack([_class_permutation(tm, dil) for dil in strided])
    out_specs = [pl.BlockSpec((3, dil, tm // dil, width),
                              lambda i: (0, i // tiles_per_seq, i % tiles_per_seq, 0))
                 for dil in dils]
    out_shape = [jax.ShapeDtypeStruct((3, batch * dil, seq // dil, width), BF16) for dil in dils]
    return pl.pallas_call(
        functools.partial(_qkv_kernel, cn=cn, dils=dils, strided=strided, width=width),
        grid=(m // tm,),
        in_specs=[pl.BlockSpec((tm, d), lambda i: (i, 0)), _resident(w_qkv.shape),
                  _resident(perms.shape)],
        out_specs=out_specs,
        out_shape=out_shape,
        scratch_shapes=[pltpu.VMEM((1 + len(strided), tm, d), BF16)],
        compiler_params=_params(1),
        name="qkv_proj",
    )(x2d, w_qkv, perms)


def _attn_kernel(q_ref, kp_ref, kc_ref, kn_ref, vp_ref, vc_ref, vn_ref, o_ref, lse_ref,
                 bias_ref, *, n_q, half, dil):
    qi = pl.program_id(1)
    tq = q_ref.shape[0]
    win = tq + 2 * half
    lanes = 2 * HEAD_DIM

    @pl.when(jnp.logical_and(pl.program_id(0) == 0, qi == 0))
    def _():
        row = lax.broadcasted_iota(jnp.int32, (tq, win), 0)
        col = lax.broadcasted_iota(jnp.int32, (tq, win), 1)
        rel = col - half - row
        dist = jnp.abs(rel)
        in_band = dist <= half
        distf = (dist * dil).astype(F32)
        for variant in range(3):
            ok = in_band
            if variant == 0:
                ok = jnp.logical_and(ok, col >= half)
            if variant == 2:
                ok = jnp.logical_and(ok, col < half + tq)
            for h in range(N_HEADS):
                slope = 2.0 ** (-8.0 * (h + 1) / N_HEADS)
                rows = slice((h % 2) * tq, (h % 2 + 1) * tq)
                bias_ref[variant, h // 2, rows, :] = jnp.where(ok, -slope * distf, NEG)

    variant = jnp.where(qi == 0, 0, jnp.where(qi == n_q - 1, 2, 1))
    kwin = jnp.concatenate([kp_ref[...], kc_ref[...], kn_ref[...]], axis=0)
    vwin = jnp.concatenate([vp_ref[...], vc_ref[...], vn_ref[...]], axis=0)
    lane = lax.broadcasted_iota(jnp.int32, (1, lanes), 1)
    first_head = lane < HEAD_DIM
    lse_lane = lax.broadcasted_iota(jnp.int32, (tq, lanes), 1)
    lse_blk = jnp.zeros((tq, lanes), F32)
    zero = jnp.zeros((), BF16)
    scale = HEAD_DIM ** -0.5
    for hp in range(N_HEADS // 2):
        cols = slice(hp * lanes, (hp + 1) * lanes)
        q2 = q_ref[:, cols] * jnp.asarray(scale, BF16)
        q_st = jnp.concatenate([jnp.where(first_head, q2, zero),
                                jnp.where(first_head, zero, q2)], axis=0)
        s = lax.dot_general(q_st, kwin[:, cols], (((1,), (1,)), ((), ())),
                            preferred_element_type=F32)
        s = s + bias_ref[variant, hp]
        m = jnp.max(s, axis=-1, keepdims=True)
        p = jnp.exp(s - m)
        den = jnp.sum(p, axis=-1, keepdims=True)
        o_st = jnp.dot(p.astype(BF16), vwin[:, cols], preferred_element_type=F32)
        o_st = o_st * (1.0 / den)
        o_ref[:, cols] = jnp.where(first_head, o_st[:tq], o_st[tq:]).astype(BF16)
        lse = m + jnp.log(den)
        lse_blk = jnp.where(lse_lane == 2 * hp, lse[:tq], lse_blk)
        lse_blk = jnp.where(lse_lane == 2 * hp + 1, lse[tq:], lse_blk)
    lse_ref[...] = lse_blk


def _band_attention(qkv, dil, half, tq=128):
    _, n_seq, length, width = qkv.shape
    n_q = length // tq
    hb = tq // half
    n_half = length // half
    assert n_q >= 2

    def cur(t):
        return pl.BlockSpec((None, None, tq, width), lambda s, i: (t, s, i, 0))

    def prev(t):
        return pl.BlockSpec((None, None, half, width),
                            lambda s, i: (t, s, jnp.maximum(i * hb - 1, 0), 0))

    def nxt(t):
        return pl.BlockSpec((None, None, half, width),
                            lambda s, i: (t, s, jnp.minimum((i + 1) * hb, n_half - 1), 0))

    o_spec = pl.BlockSpec((None, tq, width), lambda s, i: (s, i, 0))
    lse_spec = pl.BlockSpec((None, tq, LANES), lambda s, i: (s, i, 0))
    kern = functools.partial(_attn_kernel, n_q=n_q, half=half, dil=dil)
    return pl.pallas_call(
        kern,
        grid=(n_seq, n_q),
        in_specs=[cur(0), prev(1), cur(1), nxt(1), prev(2), cur(2), nxt(2)],
        out_specs=[o_spec, lse_spec],
        out_shape=[jax.ShapeDtypeStruct((n_seq, length, width), BF16),
                   jax.ShapeDtypeStruct((n_seq, length, LANES), F32)],
        scratch_shapes=[pltpu.VMEM((3, N_HEADS // 2, 2 * tq, tq + 2 * half), F32)],
        compiler_params=_params(2),
        name=f"band_attn_d{dil}",
    )(qkv, qkv, qkv, qkv, qkv, qkv, qkv)


def _merge_kernel(o0_ref, o1_ref, o2_ref, l0_ref, l1_ref, l2_ref, x_ref, e_ref, wo_ref,
                  lng_ref, lnb_ref, unperm_ref, out_ref, lscr_ref, acc_ref, *, dils, strided):
    o_refs = (o0_ref, o1_ref, o2_ref)
    l_refs = (l0_ref, l1_ref, l2_ref)
    tm, d = x_ref.shape
    o_toks = []
    for g, dil in enumerate(dils):
        if dil == 1:
            o_toks.append(None)
        else:
            o_toks.append(jnp.dot(unperm_ref[strided.index(dil)], o_refs[g][...].reshape(tm, d),
                                  preferred_element_type=F32))
    lses = []
    for g, dil in enumerate(dils):
        if dil == 1:
            lses.append(l_refs[g][0])
            continue
        for r in range(dil):
            lscr_ref[g, pl.ds(r, tm // dil, stride=dil), :] = l_refs[g][r]
        lses.append(lscr_ref[g])
    top = jnp.maximum(jnp.maximum(lses[0], lses[1]), lses[2])
    ws = [jnp.exp(l - top) for l in lses]
    inv = 1.0 / (ws[0] + ws[1] + ws[2])
    for g, dil in enumerate(dils):
        w = ws[g] * inv
        w_hi = w.astype(BF16)
        w_lo = (w - w_hi.astype(F32)).astype(BF16)
        w_full = jnp.dot(jnp.concatenate([w_hi, w_lo], axis=1), e_ref[...],
                         preferred_element_type=F32)
        o_tok = o_refs[g][0].astype(F32) if dil == 1 else o_toks[g]
        if g == 0:
            acc_ref[...] = w_full * o_tok
        else:
            acc_ref[...] += w_full * o_tok
    mix = jnp.dot(acc_ref[...].astype(BF16), wo_ref[...], preferred_element_type=F32)
    z = DN_ALPHA * x_ref[...] + mix
    out_ref[...] = _layer_norm(z, lng_ref[...], lnb_ref[...])


def _merge_out_proj(outs, lses, x2d, seq, dils, w_o, ln_g, ln_b, tm=512):
    m, d = x2d.shape
    tiles_per_seq = seq // tm
    head_of_col = jnp.arange(d) // HEAD_DIM
    expand = (jnp.arange(LANES)[:, None] == head_of_col[None, :]).astype(BF16)
    expand = jnp.concatenate([expand, expand], axis=0)
    row = pl.BlockSpec((tm, d), lambda i: (i, 0))

    def classes(dil, width):
        return pl.BlockSpec((dil, tm // dil, width),
                            lambda i: (i // tiles_per_seq, i % tiles_per_seq, 0))

    strided = tuple(sorted({dil for dil in dils if dil != 1}))
    unperms = jnp.stack([_class_permutation(tm, dil).T for dil in strided])
    return pl.pallas_call(
        functools.partial(_merge_kernel, dils=dils, strided=strided),
        grid=(m // tm,),
        in_specs=([classes(dil, d) for dil in dils] + [classes(dil, LANES) for dil in dils]
                  + [row, _resident(expand.shape), _resident(w_o.shape),
                     _resident(ln_g.shape), _resident(ln_b.shape), _resident(unperms.shape)]),
        out_specs=row,
        out_shape=jax.ShapeDtypeStruct((m, d), F32),
        scratch_shapes=[pltpu.VMEM((len(dils), tm, LANES), F32),
                        pltpu.VMEM((tm, d), F32)],
        compiler_params=_params(1),
        name="merge_out_proj",
    )(*outs, *lses, x2d, expand, w_o, ln_g, ln_b, unperms)


def kernel(x, ln_g, ln_b, rg_w_in, rg_conv_w, rg_conv_b, rg_w_a, rg_b_a, rg_w_x, rg_b_x,
           rg_lam, rg_w_out, at_w_qkv, at_w_o, ff_w_up, ff_conv_w, ff_conv_b, ff_w_down):
    batch, seq, d = x.shape
    m = batch * seq
    d_rnn = rg_w_out.shape[1]

    def vec(p):
        return p.reshape(1, -1)

    ct, kw = MXU_TILE, 2 * MXU_TILE
    k_starts = _gate_k_starts(d_rnn, LRU_BLOCKS, ct, kw)
    gate, u = _in_proj(x.reshape(m, d), seq, rg_w_in[0].astype(BF16), rg_conv_w[0],
                       vec(rg_conv_b[0]))
    gate = gate.reshape(batch, seq, d_rnn)
    u = u.reshape(batch, seq, d_rnn)
    wg_f = _gate_weights(rg_w_a[0, 0], rg_w_x[0, 0], k_starts, kw, ct)
    wg_b = _gate_weights(rg_w_a[0, 1], rg_w_x[0, 1], k_starts, kw, ct)
    h_fwd = _lru_fwd(u, wg_f, vec(0.5 * rg_b_a[0, 0]), vec(0.5 * rg_b_x[0, 0]), vec(rg_lam[0, 0]),
                     k_starts, ct=ct, kw=kw)
    x1 = _lru_bwd(u, wg_b, vec(0.5 * rg_b_a[0, 1]), vec(0.5 * rg_b_x[0, 1]), vec(rg_lam[0, 1]),
                  h_fwd, gate, x, rg_w_out[0].astype(BF16), vec(ln_g[0, 0]), vec(ln_b[0, 0]),
                  k_starts, ct=ct, kw=kw)
    x2 = _conv_ffn(x1.reshape(m, d), seq, ff_w_up[0].astype(BF16), ff_conv_w[0],
                   vec(ff_conv_b[0]), ff_w_down[0].astype(BF16), vec(ln_g[0, 1]), vec(ln_b[0, 1]))

    dils = tuple(dil for _, dil in ATTN_GROUPS)
    qkv_groups = _qkv_proj(x2, at_w_qkv[0].astype(BF16), batch, seq, dils)
    outs, lses = [], []
    for qkv_g, (window, dil) in zip(qkv_groups, ATTN_GROUPS):
        o, lse = _band_attention(qkv_g, dil, window // (2 * dil))
        outs.append(o)
        lses.append(lse)
    x3 = _merge_out_proj(outs, lses, x2, seq, dils, at_w_o[0].astype(BF16),
                         vec(ln_g[1, 0]), vec(ln_b[1, 0]))
    x4 = _conv_ffn(x3, seq, ff_w_up[1].astype(BF16), ff_conv_w[1], vec(ff_conv_b[1]),
                   ff_w_down[1].astype(BF16), vec(ln_g[1, 1]), vec(ln_b[1, 1]))
    return x4.reshape(batch, seq, d)
```

```python
import functools
import math

import jax
import jax.numpy as jnp
from jax import lax
from jax.experimental import pallas as pl
from jax.experimental.pallas import tpu as pltpu

F32 = jnp.float32
BF16 = jnp.bfloat16

LRU_C = 8.0
LRU_BLOCKS = 16
ATTN_GROUPS = ((128, 1), (512, 4), (2048, 16))
N_HEADS = 16
HEAD_DIM = 64
LN_EPS = 1e-5
DEPTH = 2
DN_ALPHA = (2 * DEPTH) ** 0.25
NEG = -1e30

VMEM_LIMIT_BYTES = 56 * 1024 * 1024
MXU_TILE = 256
LANES = 128
BF16_ROWS = 16
F32_ROWS = 8

_GELU_C = math.sqrt(2.0 / math.pi)


def _gelu(x):
    inner = x * (_GELU_C + (_GELU_C * 0.044715) * (x * x))
    return x * (0.5 + 0.5 * jnp.tanh(inner))


def _layer_norm(z, g, b):
    mu = jnp.mean(z, axis=-1, keepdims=True)
    zc = z - mu
    var = jnp.mean(zc * zc, axis=-1, keepdims=True)
    return zc * lax.rsqrt(var + LN_EPS) * g + b


def _resident(shape):
    nd = len(shape)
    return pl.BlockSpec(shape, lambda *_: (0,) * nd, pipeline_mode=pl.Buffered(1))


def _params(n_axes):
    return pltpu.CompilerParams(
        dimension_semantics=("arbitrary",) * n_axes,
        vmem_limit_bytes=VMEM_LIMIT_BYTES)


def _fill_with_halo(xe_ref, x_prev_ref, x_ref, x_next_ref, i, tiles_per_seq):
    tm = x_ref.shape[0]
    halo = x_prev_ref.shape[0]
    first = (i % tiles_per_seq == 0)
    last = (i % tiles_per_seq == tiles_per_seq - 1)
    xe_ref[0:halo, :] = (x_prev_ref[...] * jnp.where(first, 0.0, 1.0)).astype(BF16)
    xe_ref[halo:halo + tm, :] = x_ref[...].astype(BF16)
    xe_ref[halo + tm:, :] = (x_next_ref[...] * jnp.where(last, 0.0, 1.0)).astype(BF16)


def _in_proj_kernel(x_prev_ref, x_ref, x_next_ref, w_ref, cw_ref, cb_ref, gate_ref, u_ref,
                    xe_ref, *, tiles_per_seq, d_rnn, cn):
    tm = x_ref.shape[0]
    halo = x_prev_ref.shape[0]
    n = tm + 2 * halo
    _fill_with_halo(xe_ref, x_prev_ref, x_ref, x_next_ref, pl.program_id(0), tiles_per_seq)
    def project(j):
        g = jnp.dot(xe_ref[halo:halo + tm, :], w_ref[:, j * cn:(j + 1) * cn],
                    preferred_element_type=F32)
        u = jnp.dot(xe_ref[...], w_ref[:, d_rnn + j * cn:d_rnn + (j + 1) * cn],
                    preferred_element_type=F32)
        return g, u

    n_chunks = d_rnn // cn
    ahead = project(0)
    for j in range(n_chunks):
        cols = slice(j * cn, (j + 1) * cn)
        g, u = ahead
        if j + 1 < n_chunks:
            ahead = project(j + 1)
        gate_ref[:, cols] = _gelu(g).astype(BF16)
        conv = (pltpu.roll(u, 2, 0) * cw_ref[0:1, cols] + pltpu.roll(u, 1, 0) * cw_ref[1:2, cols]
                + u * cw_ref[2:3, cols] + pltpu.roll(u, n - 1, 0) * cw_ref[3:4, cols]
                + cb_ref[:, cols])
        u_ref[:, cols] = conv[halo:halo + tm].astype(BF16)


def _halo_specs(m, d, tm, halo):
    hb = tm // halo
    n_halo = m // halo
    return [pl.BlockSpec((halo, d), lambda i: (jnp.maximum(i * hb - 1, 0), 0)),
            pl.BlockSpec((tm, d), lambda i: (i, 0)),
            pl.BlockSpec((halo, d), lambda i: (jnp.minimum((i + 1) * hb, n_halo - 1), 0))]


def _in_proj(x2d, seq, w_in, conv_w, conv_b, tm=512, cn=256):
    m, d = x2d.shape
    d_rnn = w_in.shape[1] // 2
    halo = BF16_ROWS
    kern = functools.partial(_in_proj_kernel, tiles_per_seq=seq // tm, d_rnn=d_rnn, cn=cn)
    return pl.pallas_call(
        kern,
        grid=(m // tm,),
        in_specs=_halo_specs(m, d, tm, halo) + [_resident(w_in.shape), _resident(conv_w.shape),
                                                _resident(conv_b.shape)],
        out_specs=[pl.BlockSpec((tm, d_rnn), lambda i: (i, 0)),
                   pl.BlockSpec((tm, d_rnn), lambda i: (i, 0))],
        out_shape=[jax.ShapeDtypeStruct((m, d_rnn), BF16),
                   jax.ShapeDtypeStruct((m, d_rnn), BF16)],
        scratch_shapes=[pltpu.VMEM((tm + 2 * halo, d), BF16)],
        compiler_params=_params(1),
        name="in_proj",
    )(x2d, x2d, x2d, w_in, conv_w, conv_b)


def _gate_k_starts(d_rnn, n_blocks, ct, kw):
    bw = d_rnn // n_blocks
    starts = []
    for j in range(d_rnn // ct):
        lo = (j * ct) // bw * bw
        hi = ((j * ct + ct - 1) // bw + 1) * bw
        k0 = min(lo // 128 * 128, d_rnn - kw)
        assert k0 <= lo and hi <= k0 + kw
        starts.append(k0)
    return tuple(starts)


def _scan_tile(a, b, h0, reverse):
    tt, cw = a.shape
    groups = tt // F32_ROWS
    a3 = a.reshape(groups, F32_ROWS, cw)
    b3 = b.reshape(groups, F32_ROWS, cw)
    row = lax.broadcasted_iota(jnp.int32, a3.shape, 1)
    for k in (1, 2, 4):
        if reverse:
            shift, keep = F32_ROWS - k, row < F32_ROWS - k
        else:
            shift, keep = k, row >= k
        ar = pltpu.roll(a3, shift, 1)
        br = pltpu.roll(b3, shift, 1)
        b3 = b3 + a3 * jnp.where(keep, br, 0.0)
        a3 = a3 * jnp.where(keep, ar, 1.0)
    out = [None] * groups
    h = h0
    for g in (reversed(range(groups)) if reverse else range(groups)):
        hg = b3[g] + a3[g] * h
        out[g] = hg
        h = hg[0:1] if reverse else hg[F32_ROWS - 1:F32_ROWS]
    return jnp.concatenate(out, axis=0), h


def _lru_tile(u_ref, wg_ref, ba_ref, bx_ref, lam_ref, carry_ref, emit, *, reverse, k_starts,
              kw, ct):
    @pl.when(pl.program_id(1) == 0)
    def _():
        carry_ref[...] = jnp.zeros_like(carry_ref)

    lam = lam_ref[...]
    neg_lam = -lam
    softplus = jnp.maximum(neg_lam, 0.0) + jnp.log1p(jnp.exp(-jnp.abs(neg_lam)))
    half_decay = (-0.5 * LRU_C) * softplus

    for j, k0 in enumerate(k_starts):
        cols = slice(j * ct, (j + 1) * ct)
        g = jnp.dot(u_ref[:, k0:k0 + kw], wg_ref[j], preferred_element_type=F32)
        tanh_r = jnp.tanh(g[:, :ct] + ba_ref[:, cols])
        i = 0.5 + 0.5 * jnp.tanh(g[:, ct:] + bx_ref[:, cols])
        log_a = half_decay[:, cols] + half_decay[:, cols] * tanh_r
        a = jnp.exp(log_a)
        gain_sq = -jnp.tanh(log_a) * (a * a + 1.0)
        gain = jnp.where(gain_sq > 0.0, gain_sq * lax.rsqrt(gain_sq), 0.0)
        b = gain * (i * u_ref[:, cols].astype(F32))
        h, h_last = _scan_tile(a, b, carry_ref[0:1, cols], reverse)
        carry_ref[0:1, cols] = h_last
        emit(j, cols, h)


def _lru_fwd_kernel(u_ref, wg_ref, ba_ref, bx_ref, lam_ref, h_ref, carry_ref, **kw):
    def emit(j, cols, h):
        h_ref[:, cols] = h.astype(BF16)

    _lru_tile(u_ref, wg_ref, ba_ref, bx_ref, lam_ref, carry_ref, emit, reverse=False, **kw)


def _lru_bwd_kernel(u_ref, wg_ref, ba_ref, bx_ref, lam_ref, hf_ref, gate_ref, x_ref, wout_ref,
                    lng_ref, lnb_ref, out_ref, carry_ref, y_ref, **kw):
    def emit(j, cols, h):
        y = gate_ref[:, cols].astype(F32) * (hf_ref[:, cols].astype(F32) + h)
        y_ref[:, cols] = y.astype(BF16)

    _lru_tile(u_ref, wg_ref, ba_ref, bx_ref, lam_ref, carry_ref, emit, reverse=True, **kw)
    mix = jnp.dot(y_ref[...], wout_ref[...], preferred_element_type=F32)
    z = DN_ALPHA * x_ref[...] + mix
    out_ref[...] = _layer_norm(z, lng_ref[...], lnb_ref[...])


def _lru_fwd(u, wg, b_a, b_x, lam, k_starts, tt=1024, ct=256, kw=512):
    batch, seq, d_rnn = u.shape
    cur = pl.BlockSpec((None, tt, d_rnn), lambda b, i: (b, i, 0))
    kern = functools.partial(_lru_fwd_kernel, k_starts=k_starts, kw=kw, ct=ct)
    return pl.pallas_call(
        kern,
        grid=(batch, seq // tt),
        in_specs=[cur, _resident(wg.shape), _resident(b_a.shape), _resident(b_x.shape),
                  _resident(lam.shape)],
        out_specs=cur,
        out_shape=jax.ShapeDtypeStruct((batch, seq, d_rnn), BF16),
        scratch_shapes=[pltpu.VMEM((F32_ROWS, d_rnn), F32)],
        compiler_params=_params(2),
        name="lru_fwd",
    )(u, wg, b_a, b_x, lam)


def _lru_bwd(u, wg, b_a, b_x, lam, h_fwd, gate, x, w_out, ln_g, ln_b, k_starts,
             tt=1024, ct=256, kw=512):
    batch, seq, d_rnn = u.shape
    d = x.shape[-1]
    n_tiles = seq // tt
    cur = pl.BlockSpec((None, tt, d_rnn), lambda b, i: (b, n_tiles - 1 - i, 0))
    xspec = pl.BlockSpec((None, tt, d), lambda b, i: (b, n_tiles - 1 - i, 0))
    kern = functools.partial(_lru_bwd_kernel, k_starts=k_starts, kw=kw, ct=ct)
    return pl.pallas_call(
        kern,
        grid=(batch, n_tiles),
        in_specs=[cur, _resident(wg.shape), _resident(b_a.shape), _resident(b_x.shape),
                  _resident(lam.shape), cur, cur, xspec, _resident(w_out.shape),
                  _resident(ln_g.shape), _resident(ln_b.shape)],
        out_specs=xspec,
        out_shape=jax.ShapeDtypeStruct((batch, seq, d), F32),
        scratch_shapes=[pltpu.VMEM((F32_ROWS, d_rnn), F32), pltpu.VMEM((tt, d_rnn), BF16)],
        compiler_params=_params(2),
        name="lru_bwd",
    )(u, wg, b_a, b_x, lam, h_fwd, gate, x, w_out, ln_g, ln_b)


def _gate_weights(w_a, w_x, k_starts, kw, ct):
    nb, bw, _ = w_a.shape
    d_rnn = nb * bw

    def dense(w):
        w = (0.5 * w).astype(BF16)
        return jnp.concatenate(
            [jnp.pad(w[n], ((0, 0), (n * bw, d_rnn - (n + 1) * bw))) for n in range(nb)], axis=0)

    wa, wx = dense(w_a), dense(w_x)
    tiles = [jnp.concatenate([wa[k0:k0 + kw, j * ct:(j + 1) * ct],
                              wx[k0:k0 + kw, j * ct:(j + 1) * ct]], axis=1)
             for j, k0 in enumerate(k_starts)]
    return jnp.stack(tiles).astype(BF16)


def _ffn_kernel(x_prev_ref, x_ref, x_next_ref, wup_ref, cw_ref, cb_ref, wdn_ref, lng_ref,
                lnb_ref, out_ref, xe_ref, acc_ref, hm_ref, *, tiles_per_seq, d_ff, cf,
                down_every):
    tm = x_ref.shape[0]
    halo = x_prev_ref.shape[0]
    _fill_with_halo(xe_ref, x_prev_ref, x_ref, x_next_ref, pl.program_id(0), tiles_per_seq)
    n = tm + 2 * halo
    n_chunks = d_ff // cf

    def down(lo, hi):
        part = jnp.dot(hm_ref[:, lo * cf:hi * cf], wdn_ref[lo * cf:hi * cf, :],
                       preferred_element_type=F32)
        if lo == 0:
            acc_ref[...] = part
        else:
            acc_ref[...] += part

    pending = None
    start = 0
    for c in range(n_chunks):
        cols = slice(c * cf, (c + 1) * cf)
        g = jnp.dot(xe_ref[...], wup_ref[:, d_ff + c * cf:d_ff + (c + 1) * cf],
                    preferred_element_type=F32)
        v = jnp.dot(xe_ref[halo:halo + tm, :], wup_ref[:, cols], preferred_element_type=F32)
        if pending is not None:
            down(*pending)
            pending = None
        gc = (pltpu.roll(g, 1, 0) * cw_ref[0:1, cols] + g * cw_ref[1:2, cols]
              + pltpu.roll(g, n - 1, 0) * cw_ref[2:3, cols] + cb_ref[:, cols])[halo:halo + tm]
        hm_ref[:, cols] = (_gelu(gc) * v).astype(BF16)
        if (c + 1) % down_every == 0 or c + 1 == n_chunks:
            pending = (start, c + 1)
            start = c + 1
    down(*pending)
    z = DN_ALPHA * x_ref[...] + acc_ref[...]
    out_ref[...] = _layer_norm(z, lng_ref[...], lnb_ref[...])


def _conv_ffn(x2d, seq, w_up, conv_w, conv_b, w_down, ln_g, ln_b, tm=512, cf=256,
              down_every=6):
    m, d = x2d.shape
    d_ff = w_down.shape[0]
    halo = BF16_ROWS
    kern = functools.partial(_ffn_kernel, tiles_per_seq=seq // tm, d_ff=d_ff, cf=cf,
                             down_every=down_every)
    return pl.pallas_call(
        kern,
        grid=(m // tm,),
        in_specs=_halo_specs(m, d, tm, halo) + [
            _resident(w_up.shape), _resident(conv_w.shape), _resident(conv_b.shape),
            _resident(w_down.shape), _resident(ln_g.shape), _resident(ln_b.shape)],
        out_specs=pl.BlockSpec((tm, d), lambda i: (i, 0)),
        out_shape=jax.ShapeDtypeStruct((m, d), F32),
        scratch_shapes=[pltpu.VMEM((tm + 2 * halo, d), BF16), pltpu.VMEM((tm, d), F32),
                        pltpu.VMEM((tm, d_ff), BF16)],
        compiler_params=_params(1),
        name="conv_ffn",
    )(x2d, x2d, x2d, w_up, conv_w, conv_b, w_down, ln_g, ln_b)


def _class_permutation(tm, dil):
    out_row = jnp.arange(tm)
    src = (out_row % (tm // dil)) * dil + out_row // (tm // dil)
    return (src[:, None] == jnp.arange(tm)[None, :]).astype(BF16)


def _qkv_kernel(x_ref, w_ref, perm_ref, o0_ref, o1_ref, o2_ref, xb_ref, *, cn, dils, strided,
                width):
    out_refs = (o0_ref, o1_ref, o2_ref)
    tm = x_ref.shape[0]
    xb_ref[0] = x_ref[...].astype(BF16)
    lhs_of = {1: 0}
    for slot, dil in enumerate(strided, start=1):
        xb_ref[slot] = jnp.dot(perm_ref[slot - 1], xb_ref[0],
                               preferred_element_type=F32).astype(BF16)
        lhs_of[dil] = slot
    for j in range(w_ref.shape[1] // cn):
        c0 = j * cn
        g, t, off = c0 // (3 * width), (c0 % (3 * width)) // width, c0 % width
        dil = dils[g]
        res = jnp.dot(xb_ref[lhs_of[dil]], w_ref[:, c0:c0 + cn], preferred_element_type=F32)
        rows = tm // dil
        for r in range(dil):
            out_refs[g][t, r, :, off:off + cn] = res[r * rows:(r + 1) * rows].astype(BF16)


def _qkv_proj(x2d, w_qkv, batch, seq, dils, tm=512, cn=512):
    m, d = x2d.shape
    width = w_qkv.shape[1] // (3 * len(dils))
    tiles_per_seq = seq // tm
    strided = tuple(sorted({dil for dil in dils if dil != 1}))
    perms = jnp.stack([_class_permutation(tm, dil) for dil in strided])
    out_specs = [pl.BlockSpec((3, dil, tm // dil, width),
                              lambda i: (0, i // tiles_per_seq, i % tiles_per_seq, 0))
                 for dil in dils]
    out_shape = [jax.ShapeDtypeStruct((3, batch * dil, seq // dil, width), BF16) for dil in dils]
    return pl.pallas_call(
        functools.partial(_qkv_kernel, cn=cn, dils=dils, strided=strided, width=width),
        grid=(m // tm,),
        in_specs=[pl.BlockSpec((tm, d), lambda i: (i, 0)), _resident(w_qkv.shape),
                  _resident(perms.shape)],
        out_specs=out_specs,
        out_shape=out_shape,
        scratch_shapes=[pltpu.VMEM((1 + len(strided), tm, d), BF16)],
        compiler_params=_params(1),
        name="qkv_proj",
    )(x2d, w_qkv, perms)


def _attn_kernel(q_ref, kp_ref, kc_ref, kn_ref, vp_ref, vc_ref, vn_ref, o_ref, lse_ref,
                 bias_ref, *, n_q, half, dil, n_sub):
    qi = pl.program_id(1)
    tq = q_ref.shape[0] // n_sub
    win = tq + 2 * half
    lanes = 2 * HEAD_DIM

    @pl.when(jnp.logical_and(pl.program_id(0) == 0, qi == 0))
    def _():
        row = lax.broadcasted_iota(jnp.int32, (tq, win), 0)
        col = lax.broadcasted_iota(jnp.int32, (tq, win), 1)
        rel = col - half - row
        dist = jnp.abs(rel)
        in_band = dist <= half
        distf = (dist * dil).astype(F32)
        for variant in range(3):
            ok = in_band
            if variant == 0:
                ok = jnp.logical_and(ok, col >= half)
            if variant == 2:
                ok = jnp.logical_and(ok, col < half + tq)
            for h in range(N_HEADS):
                slope = 2.0 ** (-8.0 * (h + 1) / N_HEADS)
                rows = slice((h % 2) * tq, (h % 2 + 1) * tq)
                bias_ref[variant, h // 2, rows, :] = jnp.where(ok, -slope * distf, NEG)

    kall = jnp.concatenate([kp_ref[...], kc_ref[...], kn_ref[...]], axis=0)
    vall = jnp.concatenate([vp_ref[...], vc_ref[...], vn_ref[...]], axis=0)
    lane = lax.broadcasted_iota(jnp.int32, (1, lanes), 1)
    first_head = lane < HEAD_DIM
    lse_lane = lax.broadcasted_iota(jnp.int32, (tq, lanes), 1)
    zero = jnp.zeros((), BF16)
    scale = HEAD_DIM ** -0.5
    for sub in range(n_sub):
        qrows = slice(sub * tq, (sub + 1) * tq)
        kwin = kall[sub * tq:sub * tq + win]
        vwin = vall[sub * tq:sub * tq + win]
        variant = 1
        if sub == n_sub - 1:
            variant = jnp.where(qi == n_q - 1, 2, variant)
        if sub == 0:
            variant = jnp.where(qi == 0, 0, variant)
        lse_blk = jnp.zeros((tq, lanes), F32)
        for hp in range(N_HEADS // 2):
            cols = slice(hp * lanes, (hp + 1) * lanes)
            q2 = q_ref[qrows, cols] * jnp.asarray(scale, BF16)
            q_st = jnp.concatenate([jnp.where(first_head, q2, zero),
                                    jnp.where(first_head, zero, q2)], axis=0)
            s = lax.dot_general(q_st, kwin[:, cols], (((1,), (1,)), ((), ())),
                                preferred_element_type=F32)
            s = s + bias_ref[variant, hp]
            m = jnp.max(s, axis=-1, keepdims=True)
            p = jnp.exp(s - m)
            den = jnp.sum(p, axis=-1, keepdims=True)
            o_st = jnp.dot(p.astype(BF16), vwin[:, cols], preferred_element_type=F32)
            o_st = o_st * (1.0 / den)
            o_ref[qrows, cols] = jnp.where(first_head, o_st[:tq], o_st[tq:]).astype(BF16)
            lse = m + jnp.log(den)
            lse_blk = jnp.where(lse_lane == 2 * hp, lse[:tq], lse_blk)
            lse_blk = jnp.where(lse_lane == 2 * hp + 1, lse[tq:], lse_blk)
        lse_ref[qrows, :] = lse_blk


def _band_attention(qkv, dil, half, tq=128, n_sub=2):
    _, n_seq, length, width = qkv.shape
    sub_rows = tq
    tq = n_sub * sub_rows
    n_q = length // tq
    hb = tq // half
    n_half = length // half
    assert n_q * n_sub >= 2

    def cur(t):
        return pl.BlockSpec((None, None, tq, width), lambda s, i: (t, s, i, 0))

    def prev(t):
        return pl.BlockSpec((None, None, half, width),
                            lambda s, i: (t, s, jnp.maximum(i * hb - 1, 0), 0))

    def nxt(t):
        return pl.BlockSpec((None, None, half, width),
                            lambda s, i: (t, s, jnp.minimum((i + 1) * hb, n_half - 1), 0))

    o_spec = pl.BlockSpec((None, tq, width), lambda s, i: (s, i, 0))
    lse_spec = pl.BlockSpec((None, tq, LANES), lambda s, i: (s, i, 0))
    kern = functools.partial(_attn_kernel, n_q=n_q, half=half, dil=dil, n_sub=n_sub)
    return pl.pallas_call(
        kern,
        grid=(n_seq, n_q),
        in_specs=[cur(0), prev(1), cur(1), nxt(1), prev(2), cur(2), nxt(2)],
        out_specs=[o_spec, lse_spec],
        out_shape=[jax.ShapeDtypeStruct((n_seq, length, width), BF16),
                   jax.ShapeDtypeStruct((n_seq, length, LANES), F32)],
        scratch_shapes=[pltpu.VMEM((3, N_HEADS // 2, 2 * sub_rows, sub_rows + 2 * half), F32)],
        compiler_params=_params(2),
        name=f"band_attn_d{dil}",
    )(qkv, qkv, qkv, qkv, qkv, qkv, qkv)


def _merge_kernel(o0_ref, o1_ref, o2_ref, l0_ref, l1_ref, l2_ref, x_ref, e_ref, wo_ref,
                  lng_ref, lnb_ref, unperm_ref, out_ref, lscr_ref, acc_ref, *, dils, strided):
    o_refs = (o0_ref, o1_ref, o2_ref)
    l_refs = (l0_ref, l1_ref, l2_ref)
    tm, d = x_ref.shape
    o_toks = []
    for g, dil in enumerate(dils):
        if dil == 1:
            o_toks.append(None)
        else:
            o_toks.append(jnp.dot(unperm_ref[strided.index(dil)], o_refs[g][...].reshape(tm, d),
                                  preferred_element_type=F32))
    lses = []
    for g, dil in enumerate(dils):
        if dil == 1:
            lses.append(l_refs[g][0])
            continue
        for r in range(dil):
            lscr_ref[g, pl.ds(r, tm // dil, stride=dil), :] = l_refs[g][r]
        lses.append(lscr_ref[g])
    top = jnp.maximum(jnp.maximum(lses[0], lses[1]), lses[2])
    ws = [jnp.exp(l - top) for l in lses]
    inv = 1.0 / (ws[0] + ws[1] + ws[2])
    for g, dil in enumerate(dils):
        w = ws[g] * inv
        w_hi = w.astype(BF16)
        w_lo = (w - w_hi.astype(F32)).astype(BF16)
        w_full = jnp.dot(jnp.concatenate([w_hi, w_lo], axis=1), e_ref[...],
                         preferred_element_type=F32)
        o_tok = o_refs[g][0].astype(F32) if dil == 1 else o_toks[g]
        if g == 0:
            acc_ref[...] = w_full * o_tok
        else:
            acc_ref[...] += w_full * o_tok
    mix = jnp.dot(acc_ref[...].astype(BF16), wo_ref[...], preferred_element_type=F32)
    z = DN_ALPHA * x_ref[...] + mix
    out_ref[...] = _layer_norm(z, lng_ref[...], lnb_ref[...])


def _merge_out_proj(outs, lses, x2d, seq, dils, w_o, ln_g, ln_b, tm=512):
    m, d = x2d.shape
    tiles_per_seq = seq // tm
    head_of_col = jnp.arange(d) // HEAD_DIM
    expand = (jnp.arange(LANES)[:, None] == head_of_col[None, :]).astype(BF16)
    expand = jnp.concatenate([expand, expand], axis=0)
    row = pl.BlockSpec((tm, d), lambda i: (i, 0))

    def classes(dil, width):
        return pl.BlockSpec((dil, tm // dil, width),
                            lambda i: (i // tiles_per_seq, i % tiles_per_seq, 0))

    strided = tuple(sorted({dil for dil in dils if dil != 1}))
    unperms = jnp.stack([_class_permutation(tm, dil).T for dil in strided])
    return pl.pallas_call(
        functools.partial(_merge_kernel, dils=dils, strided=strided),
        grid=(m // tm,),
        in_specs=([classes(dil, d) for dil in dils] + [classes(dil, LANES) for dil in dils]
                  + [row, _resident(expand.shape), _resident(w_o.shape),
                     _resident(ln_g.shape), _resident(ln_b.shape), _resident(unperms.shape)]),
        out_specs=row,
        out_shape=jax.ShapeDtypeStruct((m, d), F32),
        scratch_shapes=[pltpu.VMEM((len(dils), tm, LANES), F32),
                        pltpu.VMEM((tm, d), F32)],
        compiler_params=_params(1),
        name="merge_out_proj",
    )(*outs, *lses, x2d, expand, w_o, ln_g, ln_b, unperms)


def kernel(x, ln_g, ln_b, rg_w_in, rg_conv_w, rg_conv_b, rg_w_a, rg_b_a, rg_w_x, rg_b_x,
           rg_lam, rg_w_out, at_w_qkv, at_w_o, ff_w_up, ff_conv_w, ff_conv_b, ff_w_down):
    batch, seq, d = x.shape
    m = batch * seq
    d_rnn = rg_w_out.shape[1]

    def vec(p):
        return p.reshape(1, -1)

    ct, kw = MXU_TILE, 2 * MXU_TILE
    k_starts = _gate_k_starts(d_rnn, LRU_BLOCKS, ct, kw)
    gate, u = _in_proj(x.reshape(m, d), seq, rg_w_in[0].astype(BF16), rg_conv_w[0],
                       vec(rg_conv_b[0]))
    gate = gate.reshape(batch, seq, d_rnn)
    u = u.reshape(batch, seq, d_rnn)
    wg_f = _gate_weights(rg_w_a[0, 0], rg_w_x[0, 0], k_starts, kw, ct)
    wg_b = _gate_weights(rg_w_a[0, 1], rg_w_x[0, 1], k_starts, kw, ct)
    h_fwd = _lru_fwd(u, wg_f, vec(0.5 * rg_b_a[0, 0]), vec(0.5 * rg_b_x[0, 0]), vec(rg_lam[0, 0]),
                     k_starts, ct=ct, kw=kw)
    x1 = _lru_bwd(u, wg_b, vec(0.5 * rg_b_a[0, 1]), vec(0.5 * rg_b_x[0, 1]), vec(rg_lam[0, 1]),
                  h_fwd, gate, x, rg_w_out[0].astype(BF16), vec(ln_g[0, 0]), vec(ln_b[0, 0]),
                  k_starts, ct=ct, kw=kw)
    x2 = _conv_ffn(x1.reshape(m, d), seq, ff_w_up[0].astype(BF16), ff_conv_w[0],
                   vec(ff_conv_b[0]), ff_w_down[0].astype(BF16), vec(ln_g[0, 1]), vec(ln_b[0, 1]))

    dils = tuple(dil for _, dil in ATTN_GROUPS)
    qkv_groups = _qkv_proj(x2, at_w_qkv[0].astype(BF16), batch, seq, dils)
    outs, lses = [], []
    for qkv_g, (window, dil) in zip(qkv_groups, ATTN_GROUPS):
        o, lse = _band_attention(qkv_g, dil, window // (2 * dil))
        outs.append(o)
        lses.append(lse)
    x3 = _merge_out_proj(outs, lses, x2, seq, dils, at_w_o[0].astype(BF16),
                         vec(ln_g[1, 0]), vec(ln_b[1, 0]))
    x4 = _conv_ffn(x3, seq, ff_w_up[1].astype(BF16), ff_conv_w[1], vec(ff_conv_b[1]),
                   ff_w_down[1].astype(BF16), vec(ln_g[1, 1]), vec(ln_b[1, 1]))
    return x4.reshape(batch, seq, d)
```
